```python
import jax, jax.numpy as jnp
from jax import lax
import numpy as np

D_MODEL = 1024
BATCH = 8
SEQ = 2048
DEPTH = 2

CTX_LEN = 256
GRID_W = 64
ROPE_BASE = 10000.0
NORM_EPS = 1e-6
NEG_INF = -1e30

A_HEADS = 8
A_KV_HEADS = 2
A_HEAD_DIM = 64
A_WIDTH = A_HEADS * A_HEAD_DIM
A_KV_WIDTH = A_KV_HEADS * A_HEAD_DIM
WINDOW = 128
A_BLOCK = 128
B_GROUPS = 8
B_WIDTH = D_MODEL // 2
B_GROUP_DIM = B_WIDTH // B_GROUPS
CHUNK = 128
EVEN_MIX = A_WIDTH + B_WIDTH
EVEN_SPLITS = [A_WIDTH, A_WIDTH + A_KV_WIDTH, A_WIDTH + 2 * A_KV_WIDTH,
               A_WIDTH + 2 * A_KV_WIDTH + B_WIDTH, A_WIDTH + 2 * A_KV_WIDTH + 2 * B_WIDTH]
EVEN_IN = A_WIDTH + 2 * A_KV_WIDTH + 2 * B_WIDTH + EVEN_MIX

C_HEADS = 16
C_NOPE = 64
C_ROPE = 32
C_V = 64
Q_RANK = 256
KV_RANK = 128
C_WIDTH = C_HEADS * C_V
ODD_SPLITS = [Q_RANK, Q_RANK + KV_RANK, Q_RANK + KV_RANK + C_ROPE]
ODD_IN = Q_RANK + KV_RANK + C_ROPE + C_WIDTH
Q_BLOCK = 128

N_EVEN = (DEPTH + 1) // 2
N_ODD = DEPTH // 2

kernel_name = 'hybrid_dit_window_gmlp_mla'


def rms_norm(x, g):
    xf = x.astype(jnp.float32)
    y = xf * lax.rsqrt(jnp.mean(xf * xf, axis=-1, keepdims=True) + NORM_EPS)
    return (y * g.astype(jnp.float32)).astype(x.dtype)


def layer_norm(x, g, b):
    xf = x.astype(jnp.float32)
    mu = jnp.mean(xf, axis=-1, keepdims=True)
    xc = xf - mu
    y = xc * lax.rsqrt(jnp.mean(xc * xc, axis=-1, keepdims=True) + NORM_EPS)
    return (y * g.astype(jnp.float32) + b.astype(jnp.float32)).astype(x.dtype)


def grid_positions(n):
    rows = n // GRID_W
    row = jnp.broadcast_to(jnp.arange(rows, dtype=jnp.float32)[:, None], (rows, GRID_W)).reshape(-1)
    col = jnp.broadcast_to(jnp.arange(GRID_W, dtype=jnp.float32)[None, :], (rows, GRID_W)).reshape(-1)
    return row, col


def rope_1d(x, pos):
    half = x.shape[-1] // 2
    inv = ROPE_BASE ** (-jnp.arange(half, dtype=jnp.float32) / half)
    ang = pos[:, None] * inv[None, :]
    cos = jnp.cos(ang)[None, :, None, :]
    sin = jnp.sin(ang)[None, :, None, :]
    x1, x2 = x[..., :half], x[..., half:]
    return jnp.concatenate([x1 * cos - x2 * sin, x1 * sin + x2 * cos], axis=-1)


def rope_2d(x, row, col):
    d = x.shape[-1] // 2
    xf = x.astype(jnp.float32)
    out = jnp.concatenate([rope_1d(xf[..., :d], row), rope_1d(xf[..., d:], col)], axis=-1)
    return out.astype(x.dtype)


def windowed_sink_gqa(q, k, v, kc, vc, sink):
    B_, S, Hq, Dh = q.shape
    Hkv = k.shape[2]
    G = Hq // Hkv
    blk = A_BLOCK
    nb = S // blk
    C = kc.shape[1]
    scale = Dh ** -0.5
    qb = q.reshape(B_, nb, blk, Hkv, G, Dh)
    pad = ((0, 0), (blk, blk), (0, 0), (0, 0))
    kp = jnp.pad(k, pad).reshape(B_, nb + 2, blk, Hkv, Dh)
    vp = jnp.pad(v, pad).reshape(B_, nb + 2, blk, Hkv, Dh)
    kb = jnp.concatenate([kp[:, :-2], kp[:, 1:-1], kp[:, 2:]], axis=2)
    vb = jnp.concatenate([vp[:, :-2], vp[:, 1:-1], vp[:, 2:]], axis=2)
    qi = jnp.arange(nb)[:, None, None] * blk + jnp.arange(blk)[None, :, None]
    kj = jnp.arange(nb)[:, None, None] * blk - blk + jnp.arange(3 * blk)[None, None, :]
    valid = (jnp.abs(qi - kj) <= WINDOW) & (kj >= 0) & (kj < S)
    s_loc = jnp.einsum('bnqhgd,bnkhd->bhgnqk', qb, kb).astype(jnp.float32) * scale
    s_loc = jnp.where(valid, s_loc, NEG_INF)
    s_ctx = jnp.einsum('bnqhgd,bchd->bhgnqc', qb, kc).astype(jnp.float32) * scale
    s_sink = jnp.broadcast_to(sink.astype(jnp.float32).reshape(1, Hkv, G, 1, 1, 1), s_loc.shape[:-1] + (1,))
    p = jax.nn.softmax(jnp.concatenate([s_loc, s_ctx, s_sink], axis=-1), axis=-1).astype(v.dtype)
    nk = 3 * blk
    o = (jnp.einsum('bhgnqk,bnkhd->bnqhgd', p[..., :nk], vb)
         + jnp.einsum('bhgnqc,bchd->bnqhgd', p[..., nk:nk + C], vc))
    return o.reshape(B_, S, Hq * Dh)


def context_sink_gqa(qc, kc, vc, sink):
    B_, C, Hq, Dh = qc.shape
    Hkv = kc.shape[2]
    G = Hq // Hkv
    scale = Dh ** -0.5
    qg = qc.reshape(B_, C, Hkv, G, Dh)
    s = jnp.einsum('bqhgd,bkhd->bhgqk', qg, kc).astype(jnp.float32) * scale
    s_sink = jnp.broadcast_to(sink.astype(jnp.float32).reshape(1, Hkv, G, 1, 1), s.shape[:-1] + (1,))
    p = jax.nn.softmax(jnp.concatenate([s, s_sink], axis=-1), axis=-1)[..., :C].astype(vc.dtype)
    return jnp.einsum('bhgqk,bkhd->bqhgd', p, vc).reshape(B_, C, Hq * Dh)


def chunk_gmlp(u, v, ln_g, ln_b, w_s, b_s):
    B_, L, _ = v.shape
    n = L // CHUNK
    vn = layer_norm(v, ln_g, ln_b).reshape(B_, n, CHUNK, B_GROUPS, B_GROUP_DIM)
    s = jnp.einsum('gpq,bnqgc->bnpgc', w_s, vn) + jnp.transpose(b_s)[None, None, :, :, None]
    return u * s.reshape(B_, L, B_WIDTH)


def even_layer(hl, hc, w_in, sink, ln_g, ln_b, w_s, b_s, w_out, row, col, need_ctx):
    B_, S, _ = hl.shape
    C = hc.shape[1]
    gelu = lambda t: jax.nn.gelu(t, approximate=False)
    q, k, v, u, vv, gate = jnp.split(hl @ w_in, EVEN_SPLITS, axis=-1)
    q = rope_2d(q.reshape(B_, S, A_HEADS, A_HEAD_DIM), row, col)
    k = rope_2d(k.reshape(B_, S, A_KV_HEADS, A_HEAD_DIM), row, col)
    v = v.reshape(B_, S, A_KV_HEADS, A_HEAD_DIM)
    if need_ctx:
        qc, kc, vc, uc, vvc, gate_c = jnp.split(hc @ w_in, EVEN_SPLITS, axis=-1)
    else:
        kc, vc = jnp.split(hc @ w_in[:, A_WIDTH:A_WIDTH + 2 * A_KV_WIDTH], 2, axis=-1)
    kc = kc.reshape(B_, C, A_KV_HEADS, A_HEAD_DIM)
    vc = vc.reshape(B_, C, A_KV_HEADS, A_HEAD_DIM)
    a = windowed_sink_gqa(q, k, v, kc, vc, sink)
    b = chunk_gmlp(gelu(u), gelu(vv), ln_g, ln_b, w_s, b_s)
    y = (jnp.concatenate([a, b], axis=-1) * jax.nn.silu(gate)) @ w_out
    if need_ctx:
        ac = context_sink_gqa(qc.reshape(B_, C, A_HEADS, A_HEAD_DIM), kc, vc, sink)
        bc = chunk_gmlp(gelu(uc), gelu(vvc), ln_g, ln_b, w_s, b_s)
        yc = (jnp.concatenate([ac, bc], axis=-1) * jax.nn.silu(gate_c)) @ w_out
    else:
        yc = None
    return y, yc


def mla_queries(q_a, q_norm, w_qb, row, col):
    B_, L, _ = q_a.shape
    q = (rms_norm(q_a, q_norm) @ w_qb).reshape(B_, L, C_HEADS, C_NOPE + C_ROPE)
    q_nope, q_pe = q[..., :C_NOPE], q[..., C_NOPE:]
    if row is not None:
        q_pe = rope_2d(q_pe, row, col)
    return jnp.concatenate([q_nope, q_pe], axis=-1)


def mla_keys_values(kv_a, k_pe, kv_norm, w_kvb, row, col):
    B_, L, _ = kv_a.shape
    kv = (rms_norm(kv_a, kv_norm) @ w_kvb).reshape(B_, L, C_HEADS, C_NOPE + C_V)
    k_nope, v = kv[..., :C_NOPE], kv[..., C_NOPE:]
    k_pe = k_pe[:, :, None, :]
    if row is not None:
        k_pe = rope_2d(k_pe, row, col)
    k = jnp.concatenate([k_nope, jnp.broadcast_to(k_pe, (B_, L, C_HEADS, C_ROPE))], axis=-1)
    return k, v


def block_attention(q, k, v):
    B_, Lq, H, dk = q.shape
    nb = Lq // Q_BLOCK
    scale = dk ** -0.5
    qb = jnp.moveaxis(q.reshape(B_, nb, Q_BLOCK, H, dk), 1, 0)

    def one(qblk):
        s = jnp.einsum('bqhd,bkhd->bhqk', qblk, k).astype(jnp.float32) * scale
        p = jax.nn.softmax(s, axis=-1).astype(v.dtype)
        return jnp.einsum('bhqk,bkhd->bqhd', p, v)

    o = lax.map(one, qb)
    return jnp.moveaxis(o, 0, 1).reshape(B_, Lq, H * v.shape[-1])


def odd_layer(hl, hc, w_in, q_norm, w_qb, kv_norm, w_kvb, w_out, row, col, need_ctx):
    q_a, kv_a, k_pe, gate = jnp.split(hl @ w_in, ODD_SPLITS, axis=-1)
    q = mla_queries(q_a, q_norm, w_qb, row, col)
    k, v = mla_keys_values(kv_a, k_pe, kv_norm, w_kvb, row, col)
    if need_ctx:
        q_ac, kv_ac, k_pec, gate_c = jnp.split(hc @ w_in, ODD_SPLITS, axis=-1)
    else:
        kv_ac, k_pec = jnp.split(hc @ w_in[:, Q_RANK:Q_RANK + KV_RANK + C_ROPE], [KV_RANK], axis=-1)
    kc, vc = mla_keys_values(kv_ac, k_pec, kv_norm, w_kvb, None, None)
    o = block_attention(q, jnp.concatenate([k, kc], axis=1), jnp.concatenate([v, vc], axis=1))
    y = (o * jax.nn.silu(gate)) @ w_out
    if need_ctx:
        qc = mla_queries(q_ac, q_norm, w_qb, None, None)
        oc = block_attention(qc, kc, vc)
        yc = (oc * jax.nn.silu(gate_c)) @ w_out
    else:
        yc = None
    return y, yc


def setup_inputs(seed: int = 0) -> dict:
    key = jax.random.key(seed)
    ks = jax.random.split(key, 24)
    f32 = jnp.float32

    def nrm(k, shape, s=1.0):
        return jax.random.normal(k, shape, f32) * s

    def w(k, shape, fan_in, g=1.0):
        return nrm(k, shape, g * fan_in ** -0.5)

    def gain(k, shape):
        return 1.0 + nrm(k, shape, 0.05)

    return {
        'x': nrm(ks[0], (BATCH, SEQ, D_MODEL)),
        'c': nrm(ks[1], (BATCH, D_MODEL)),
        'ctx': nrm(ks[2], (BATCH, CTX_LEN, D_MODEL)),
        'c_ctx': nrm(ks[3], (D_MODEL,)),
        'norm_g': gain(ks[4], (DEPTH, D_MODEL)),
        'w_ada': w(ks[5], (DEPTH, D_MODEL, 3 * D_MODEL), D_MODEL, 0.5),
        'b_ada': nrm(ks[6], (DEPTH, 3 * D_MODEL), 0.02),
        'w_in0': w(ks[7], (N_EVEN, D_MODEL, EVEN_IN), D_MODEL),
        'sink0': nrm(ks[8], (N_EVEN, A_HEADS), 0.5),
        'gm_ln_g': gain(ks[9], (N_EVEN, B_WIDTH)),
        'gm_ln_b': nrm(ks[10], (N_EVEN, B_WIDTH), 0.02),
        'gm_ws': w(ks[11], (N_EVEN, B_GROUPS, CHUNK, CHUNK), CHUNK),
        'gm_bs': 1.0 + nrm(ks[12], (N_EVEN, B_GROUPS, CHUNK), 0.02),
        'w_out0': w(ks[13], (N_EVEN, EVEN_MIX, D_MODEL), EVEN_MIX),
        'w_in1': w(ks[14], (N_ODD, D_MODEL, ODD_IN), D_MODEL),
        'q_norm': gain(ks[15], (N_ODD, Q_RANK)),
        'w_qb': w(ks[16], (N_ODD, Q_RANK, C_HEADS * (C_NOPE + C_ROPE)), Q_RANK),
        'kv_norm': gain(ks[17], (N_ODD, KV_RANK)),
        'w_kvb': w(ks[18], (N_ODD, KV_RANK, C_HEADS * (C_NOPE + C_V)), KV_RANK),
        'w_out1': w(ks[19], (N_ODD, C_WIDTH, D_MODEL), C_WIDTH),
        'final_g': gain(ks[20], (D_MODEL,)),
    }


def reference(x, c, ctx, c_ctx, norm_g, w_ada, b_ada, w_in0, sink0, gm_ln_g, gm_ln_b, gm_ws, gm_bs,
              w_out0, w_in1, q_norm, w_qb, kv_norm, w_kvb, w_out1, final_g):
    S = x.shape[1]
    row, col = grid_positions(S)
    cond_l = jax.nn.silu(c)[:, None, :]
    cond_c = jax.nn.silu(c_ctx)[None, None, :]
    xc = ctx
    for layer in range(DEPTH):
        need_ctx = layer < DEPTH - 1
        shift_l, scale_l, gate_l = jnp.split(cond_l @ w_ada[layer] + b_ada[layer], 3, axis=-1)
        shift_c, scale_c, gate_c = jnp.split(cond_c @ w_ada[layer] + b_ada[layer], 3, axis=-1)
        hl = rms_norm(x, norm_g[layer]) * (1.0 + scale_l) + shift_l
        hc = rms_norm(xc, norm_g[layer]) * (1.0 + scale_c) + shift_c
        i = layer // 2
        if layer % 2 == 0:
            yl, yc = even_layer(hl, hc, w_in0[i], sink0[i], gm_ln_g[i], gm_ln_b[i], gm_ws[i], gm_bs[i],
                                w_out0[i], row, col, need_ctx)
        else:
            yl, yc = odd_layer(hl, hc, w_in1[i], q_norm[i], w_qb[i], kv_norm[i], w_kvb[i], w_out1[i],
                               row, col, need_ctx)
        x = x + gate_l * yl
        if need_ctx:
            xc = xc + gate_c * yc
    return rms_norm(x, final_g)
```

```python
import functools

import numpy as np
import jax
import jax.numpy as jnp
from jax import lax
from jax.experimental import pallas as pl
from jax.experimental.pallas import tpu as pltpu

D_MODEL = 1024
GRID_W = 64
ROPE_BASE = 10000.0
NORM_EPS = 1e-6
NEG_INF = -1e30

A_HEADS = 8
A_KV_HEADS = 2
A_HEAD_DIM = 64
A_WIDTH = A_HEADS * A_HEAD_DIM
A_KV_WIDTH = A_KV_HEADS * A_HEAD_DIM
A_BLOCK = 128
B_GROUPS = 8
B_WIDTH = 512
B_GROUP_DIM = 64
EVEN_MIX = A_WIDTH + B_WIDTH

C_HEADS = 16
C_NOPE = 64
C_ROPE = 32
C_V = 64
Q_RANK = 256
KV_RANK = 128
C_WIDTH = C_HEADS * C_V

LANES = 128
VMEM_LIMIT = 48 * 1024 * 1024

BF16 = jnp.bfloat16
F32 = jnp.float32


def _params(n_axes):
    return pltpu.CompilerParams(dimension_semantics=("parallel",) * n_axes,
                                vmem_limit_bytes=VMEM_LIMIT)


def _dot(a, b):
    return jnp.dot(a, b, preferred_element_type=F32)


def _dot_nt(a, b):
    return lax.dot_general(a, b, (((1,), (1,)), ((), ())), preferred_element_type=F32)


def _rope(x, cos, sin_hi, sin_lo, half):
    return (x * cos + pltpu.roll(x, half, 1) * sin_hi
            + pltpu.roll(x, LANES - half, 1) * sin_lo)


def _gelu(x):
    return 0.5 * x * (1.0 + lax.erf(x * (2.0 ** -0.5)))


def _modulated_rms(x, g, scale, shift):
    r = lax.rsqrt(jnp.mean(x * x, axis=-1, keepdims=True) + NORM_EPS)
    return (x * r) * (g * (1.0 + scale)) + shift


def _ada_body(c_ref, w_ref, b_ref, o_ref):
    c = c_ref[...]
    cs = c * jax.nn.sigmoid(c)
    o_ref[0] = _dot(cs, w_ref[0]) + b_ref[0]


def _ada_call(cond, w_ada, b_ada):
    depth, d, n3 = w_ada.shape
    rows = cond.shape[0]
    tn = 768
    return pl.pallas_call(
        _ada_body,
        grid=(depth, n3 // tn),
        in_specs=[pl.BlockSpec((rows, d), lambda l, j: (0, 0)),
                  pl.BlockSpec((1, d, tn), lambda l, j: (l, 0, j)),
                  pl.BlockSpec((1, 1, tn), lambda l, j: (l, 0, j))],
        out_specs=pl.BlockSpec((1, rows, tn), lambda l, j: (l, 0, j)),
        out_shape=jax.ShapeDtypeStruct((depth, rows, n3), F32),
        compiler_params=_params(2),
        name="ada",
    )(cond, w_ada, b_ada.reshape(depth, 1, n3))


def _in0_body(use_rope, x_ref, g_ref, sc_ref, sh_ref, wq_ref, wkv_ref, wu_ref, wvv_ref, wg_ref,
              lng_ref, lnb_ref, cos_ref, shi_ref, slo_ref,
              q_ref, k_ref, v_ref, gu_ref, vn_ref, sg_ref):
    h = _modulated_rms(x_ref[0], g_ref[...], sc_ref[0], sh_ref[0]).astype(BF16)
    if use_rope:
        cos, shi, slo = cos_ref[...], shi_ref[...], slo_ref[...]
        rope = lambda t: _rope(t, cos, shi, slo, 16)
    else:
        rope = lambda t: t
    scale = A_HEAD_DIM ** -0.5
    q = _dot(h, wq_ref[...])
    for j in range(A_WIDTH // LANES):
        sl = slice(j * LANES, (j + 1) * LANES)
        q_ref[0, :, sl] = (rope(q[:, sl]) * scale).astype(BF16)
    kv = _dot(h, wkv_ref[...])
    k_ref[0] = rope(kv[:, :A_KV_WIDTH]).astype(BF16)
    v_ref[0] = kv[:, A_KV_WIDTH:].astype(BF16)
    gu_ref[0] = _gelu(_dot(h, wu_ref[...])).astype(BF16)
    gv = _gelu(_dot(h, wvv_ref[...]))
    mu = jnp.mean(gv, axis=-1, keepdims=True)
    gc = gv - mu
    vn = gc * lax.rsqrt(jnp.mean(gc * gc, axis=-1, keepdims=True) + NORM_EPS)
    vn_ref[0] = (vn * lng_ref[...] + lnb_ref[...]).astype(BF16)
    gate = _dot(h, wg_ref[...])
    sg_ref[0] = (gate * jax.nn.sigmoid(gate)).astype(BF16)


def _in0_call(x, g, scale, shift, w, tabs, tm, use_rope):
    b, s, d = x.shape
    nmod = scale.shape[0]
    mod_map = (lambda i, j: (i, 0, 0)) if nmod > 1 else (lambda i, j: (0, 0, 0))
    full = lambda a: pl.BlockSpec(a.shape, lambda i, j: (0,) * a.ndim)
    row = lambda n: pl.BlockSpec((1, tm, n), lambda i, j: (i, j, 0))
    tab = pl.BlockSpec((tm, LANES), lambda i, j: (j, 0))
    widths =[A_WIDTH, A_KV_WIDTH, A_KV_WIDTH, B_WIDTH, B_WIDTH, EVEN_MIX]
    return pl.pallas_call(
        functools.partial(_in0_body, use_rope),
        grid=(b, s // tm),
        in_specs=[row(d), full(g), pl.BlockSpec((1, 1, d), mod_map), pl.BlockSpec((1, 1, d), mod_map),
                  full(w["wq"]), full(w["wkv"]), full(w["wu"]), full(w["wvv"]), full(w["wg"]),
                  full(w["lng"]), full(w["lnb"]), tab, tab, tab],
        out_specs=[row(n) for n in widths],
        out_shape=[jax.ShapeDtypeStruct((b, s, n), BF16) for n in widths],
        compiler_params=_params(2),
        name="in0_lat" if use_rope else "in0_ctx",
    )(x, g, scale, shift, w["wq"], w["wkv"], w["wu"], w["wvv"], w["wg"], w["lng"], w["lnb"], *tabs)


def _softmax_av(scores, values, sink):
    m = sink
    for s in scores:
        m = jnp.maximum(m, jnp.max(s, axis=-1, keepdims=True))
    den = jnp.exp(sink - m)
    acc = None
    for s, v in zip(scores, values):
        p = jnp.exp(s - m)
        den = den + jnp.sum(p, axis=-1, keepdims=True)
        pv = _dot(p.astype(BF16), v)
        acc = pv if acc is None else acc + pv
    return acc / den


def _mix0_body(has_local, nchunk, sink_ref, q_ref, *refs):
    if has_local:
        kp_ref, kc_ref, kn_ref, vp_ref, vc_ref, vn_ref, *refs = refs
    kx_ref, vx_ref, gu_ref, vnrm_ref, sg_ref, x_ref, gl_ref, ws_ref, bs_ref, wo_ref, o_ref = refs
    rows = q_ref.shape[1]
    lane_lo = lax.broadcasted_iota(jnp.int32, (rows, LANES), 1) < A_HEAD_DIM
    chunk_lane_lo = lax.broadcasted_iota(jnp.int32, (A_BLOCK, LANES), 1) < A_HEAD_DIM
    if has_local:
        i = pl.program_id(1)
        nblk = pl.num_programs(1)
        r = lax.broadcasted_iota(jnp.int32, (rows, A_BLOCK), 0)
        c = lax.broadcasted_iota(jnp.int32, (rows, A_BLOCK), 1)
        mask_prev = (c >= r) & (i > 0)
        mask_next = (c <= r) & (i < nblk - 1)
        k_loc = (kp_ref[0], kc_ref[0], kn_ref[0])
        v_loc = (vp_ref[0], vc_ref[0], vn_ref[0])
    kx = kx_ref[0]
    vx = vx_ref[0]

    pieces = []
    npair = A_WIDTH // LANES
    for jp in range(npair):
        qp = q_ref[0, :, jp * LANES:(jp + 1) * LANES]
        outs = []
        for side in range(2):
            head = jp + npair * side
            qm = jnp.where(lane_lo if side == 0 else ~lane_lo, qp, jnp.zeros_like(qp))
            scores, values = [], []
            if has_local:
                s_prev = jnp.where(mask_prev, _dot_nt(qm, k_loc[0]), NEG_INF)
                s_cur = _dot_nt(qm, k_loc[1])
                s_next = jnp.where(mask_next, _dot_nt(qm, k_loc[2]), NEG_INF)
                scores += [s_prev, s_cur, s_next]
                values += list(v_loc)
            scores.append(_dot_nt(qm, kx))
            values.append(vx)
            outs.append(_softmax_av(scores, values, sink_ref[head]))
        pieces.append(jnp.where(lane_lo, outs[0], outs[1]))

    for jp in range(B_WIDTH // LANES):
        sl = slice(jp * LANES, (jp + 1) * LANES)
        chunks = []
        for ch in range(nchunk):
            rs = slice(ch * A_BLOCK, (ch + 1) * A_BLOCK)
            vp = vnrm_ref[0, rs, sl]
            s0 = _dot(ws_ref[2 * jp], vp)
            s1 = _dot(ws_ref[2 * jp + 1], vp)
            chunks.append(jnp.where(chunk_lane_lo, s0, s1) + bs_ref[:, sl])
        sg = chunks[0] if nchunk == 1 else jnp.concatenate(chunks, axis=0)
        pieces.append(gu_ref[0, :, sl].astype(F32) * sg)

    mixed = jnp.concatenate(pieces, axis=-1) * sg_ref[0].astype(F32)
    y = _dot(mixed.astype(BF16), wo_ref[...])
    o_ref[0] = x_ref[0] + gl_ref[0] * y


def _mix0_call(sink, q, k, v, kx, vx, gu, vnrm, sg, x, gate, ws, bs, wo, has_local):
    b, s, d = x.shape
    rows = A_BLOCK if has_local else s
    nchunk = rows // A_BLOCK
    nblk = s // rows
    nmod = gate.shape[0]
    mod_map = (lambda i, j: (i, 0, 0)) if nmod > 1 else (lambda i, j: (0, 0, 0))
    full = lambda a: pl.BlockSpec(a.shape, lambda i, j: (0,) * a.ndim)
    row = lambda n: pl.BlockSpec((1, rows, n), lambda i, j: (i, j, 0))
    ctx = lambda a: pl.BlockSpec((1,) + a.shape[1:], lambda i, j: (i, 0, 0))
    in_specs = [pl.BlockSpec(memory_space=pltpu.SMEM), row(A_WIDTH)]
    args = [sink, q]
    if has_local:
        blk = lambda f: pl.BlockSpec((1, A_BLOCK, A_KV_WIDTH), lambda i, j: (i, f(j), 0))
        nbr = [blk(lambda j: jnp.maximum(j - 1, 0)), blk(lambda j: j),
               blk(lambda j: jnp.minimum(j + 1, nblk - 1))]
        in_specs += nbr + nbr
        args += [k, k, k, v, v, v]
    in_specs += [ctx(kx), ctx(vx), row(B_WIDTH), row(B_WIDTH), row(EVEN_MIX), row(d),
                 pl.BlockSpec((1, 1, d), mod_map), full(ws), full(bs), full(wo)]
    args += [kx, vx, gu, vnrm, sg, x, gate, ws, bs, wo]
    return pl.pallas_call(
        functools.partial(_mix0_body, has_local, nchunk),
        grid=(b, nblk),
        in_specs=in_specs,
        out_specs=row(d),
        out_shape=jax.ShapeDtypeStruct((b, s, d), F32),
        compiler_params=_params(2),
        name="mix0_lat" if has_local else "mix0_ctx",
    )(*args)


def _rms(x, g):
    return x * lax.rsqrt(jnp.mean(x * x, axis=-1, keepdims=True) + NORM_EPS) * g


def _in1_kv(h, use_rope, wkv_ref, kvn_ref, wk_ref, wvt_ref, tabs, k_ref, vt_ref):
    kva = _dot(h, wkv_ref[...])
    kvn = _rms(kva[:, :KV_RANK], kvn_ref[...]).astype(BF16)
    kpe = kva[:, KV_RANK:]
    if use_rope:
        kpe = _rope(kpe, *tabs, 8)
    kin = jnp.concatenate([kvn, kpe.astype(BF16)], axis=-1)
    k_ref[0] = _dot(kin, wk_ref[...]).astype(BF16)
    vt_ref[0] = _dot_nt(wvt_ref[...], kvn).astype(BF16)


def _in1_lat_body(x_ref, g_ref, sc_ref, sh_ref, wqa_ref, qn_ref, wqb_ref, wkv_ref, kvn_ref, wk_ref, wvt_ref,
                  wg_ref, cos_ref, shi_ref, slo_ref, q_ref, k_ref, vt_ref, sg_ref):
    h = _modulated_rms(x_ref[0], g_ref[...], sc_ref[0], sh_ref[0]).astype(BF16)
    tabs = (cos_ref[...], shi_ref[...], slo_ref[...])
    qn = _rms(_dot(h, wqa_ref[...]), qn_ref[...]).astype(BF16)
    scale = (C_NOPE + C_ROPE) ** -0.5
    for hp in range(C_HEADS // 2):
        q2 = _dot(qn, wqb_ref[:, hp * 2 * LANES:(hp + 1) * 2 * LANES])
        for t in range(2):
            sl = slice(t * LANES, (t + 1) * LANES)
            q_ref[0, :, (2 * hp + t) * LANES:(2 * hp + t + 1) * LANES] = (
                _rope(q2[:, sl], *tabs, 8) * scale).astype(BF16)
    _in1_kv(h, True, wkv_ref, kvn_ref, wk_ref, wvt_ref, tabs, k_ref, vt_ref)
    gate = _dot(h, wg_ref[...])
    sg_ref[0] = (gate * jax.nn.sigmoid(gate)).astype(BF16)


def _in1_ctx_body(x_ref, g_ref, sc_ref, sh_ref, wkv_ref, kvn_ref, wk_ref, wvt_ref, k_ref, vt_ref):
    h = _modulated_rms(x_ref[0], g_ref[...], sc_ref[0], sh_ref[0]).astype(BF16)
    _in1_kv(h, False, wkv_ref, kvn_ref, wk_ref, wvt_ref, None, k_ref, vt_ref)


def _in1_call(x, g, scale, shift, w, tabs, tm, latent):
    b, s, d = x.shape
    nmod = scale.shape[0]
    mod_map = (lambda i, j: (i, 0, 0)) if nmod > 1 else (lambda i, j: (0, 0, 0))
    full = lambda a: pl.BlockSpec(a.shape, lambda i, j: (0,) * a.ndim)
    row = lambda n: pl.BlockSpec((1, tm, n), lambda i, j: (i, j, 0))
    tab = pl.BlockSpec((tm, LANES), lambda i, j: (j, 0))
    mod = pl.BlockSpec((1, 1, d), mod_map)
    kw = C_HEADS * LANES
    k_shape = jax.ShapeDtypeStruct((b, s, kw), BF16)
    vt_shape = jax.ShapeDtypeStruct((b, C_WIDTH, s), BF16)
    vt_spec = pl.BlockSpec((1, C_WIDTH, tm), lambda i, j: (i, 0, j))
    kv_w = [w["wkv"], w["kvn"], w["wk"], w["wvt"]]
    if latent:
        return pl.pallas_call(
            _in1_lat_body,
            grid=(b, s // tm),
            in_specs=[row(d), full(g), mod, mod, full(w["wqa"]), full(w["qn"]), full(w["wqb"])]
                     + [full(a) for a in kv_w] + [full(w["wg"]), tab, tab, tab],
            out_specs=[row(kw), row(kw), vt_spec, row(C_WIDTH)],
            out_shape=[k_shape, k_shape, vt_shape, jax.ShapeDtypeStruct((b, s, C_WIDTH), BF16)],
            compiler_params=_params(2),
            name="in1_lat",
        )(x, g, scale, shift, w["wqa"], w["qn"], w["wqb"], *kv_w, w["wg"], *tabs)
    return pl.pallas_call(
        _in1_ctx_body,
        grid=(b, s // tm),
        in_specs=[row(d), full(g), mod, mod] + [full(a) for a in kv_w],
        out_specs=[row(kw), vt_spec],
        out_shape=[k_shape, vt_shape],
        compiler_params=_params(2),
        name="in1_ctx",
    )(x, g, scale, shift, *kv_w)


def _attn1_body(q_ref, k_ref, kx_ref, vt_ref, vxt_ref, sg_ref, o_ref):
    outs = []
    for t in range(2):
        sl = slice(t * LANES, (t + 1) * LANES)
        q = q_ref[0, :, sl]
        s_l = _dot_nt(k_ref[0, :, sl], q)
        s_x = _dot_nt(kx_ref[0, :, sl], q)
        m = jnp.maximum(jnp.max(s_l, axis=0, keepdims=True), jnp.max(s_x, axis=0, keepdims=True))
        p_l = jnp.exp(s_l - m)
        p_x = jnp.exp(s_x - m)
        den = jnp.sum(p_l, axis=0, keepdims=True) + jnp.sum(p_x, axis=0, keepdims=True)
        vs = slice(t * C_V, (t + 1) * C_V)
        o_t = _dot(vt_ref[0, vs, :], p_l.astype(BF16)) + _dot(vxt_ref[0, vs, :], p_x.astype(BF16))
        outs.append(o_t / den)
    o = jnp.concatenate(outs, axis=0).T
    o_ref[0] = (o * sg_ref[0].astype(F32)).astype(BF16)


def _attn1_call(q, k, kx, vt, vxt, sg, tq):
    b, s, _ = q.shape
    sx = kx.shape[1]
    npair = C_HEADS // 2
    return pl.pallas_call(
        _attn1_body,
        grid=(b, npair, s // tq),
        in_specs=[pl.BlockSpec((1, tq, 2 * LANES), lambda i, h, j: (i, j, h)),
                  pl.BlockSpec((1, s, 2 * LANES), lambda i, h, j: (i, 0, h)),
                  pl.BlockSpec((1, sx, 2 * LANES), lambda i, h, j: (i, 0, h)),
                  pl.BlockSpec((1, 2 * C_V, s), lambda i, h, j: (i, h, 0)),
                  pl.BlockSpec((1, 2 * C_V, sx), lambda i, h, j: (i, h, 0)),
                  pl.BlockSpec((1, tq, 2 * C_V), lambda i, h, j: (i, j, h))],
        out_specs=pl.BlockSpec((1, tq, 2 * C_V), lambda i, h, j: (i, j, h)),
        out_shape=jax.ShapeDtypeStruct((b, s, C_WIDTH), BF16),
        compiler_params=_params(3),
        name="attn1",
    )(q, k, kx, vt, vxt, sg)


def _out1_body(og_ref, x_ref, gl_ref, wo_ref, fg_ref, o_ref):
    x = x_ref[0] + gl_ref[0] * _dot(og_ref[0], wo_ref[...])
    o_ref[0] = _rms(x, fg_ref[...])


def _out1_call(og, x, gate, wo, fg, tm):
    b, s, d = x.shape
    full = lambda a: pl.BlockSpec(a.shape, lambda i, j: (0,) * a.ndim)
    row = lambda n: pl.BlockSpec((1, tm, n), lambda i, j: (i, j, 0))
    return pl.pallas_call(
        _out1_body,
        grid=(b, s // tm),
        in_specs=[row(C_WIDTH), row(d), pl.BlockSpec((1, 1, d), lambda i, j: (i, 0, 0)), full(wo), full(fg)],
        out_specs=row(d),
        out_shape=jax.ShapeDtypeStruct((b, s, d), F32),
        compiler_params=_params(2),
        name="out1",
    )(og, x, gate, wo, fg)


def _rope_tables(seq, offset, width, half):
    t = np.arange(seq)
    pos = np.stack([t // GRID_W, t % GRID_W], axis=0).astype(np.float32)
    lane = np.arange(LANES)
    d = (lane - offset) % (2 * width) if offset else lane % (2 * width)
    active = np.ones(LANES, bool) if not offset else (lane >= offset) & (lane < offset + 2 * width)
    axis = d // width
    e = d % width
    f = e % half
    inv = (ROPE_BASE ** (-(np.arange(half, dtype=np.float32)) / half)).astype(np.float32)
    ang = pos[axis, :].T * inv[f][None, :]
    cos = np.where(active[None, :], np.cos(ang), 1.0)
    sin = np.where(active[None, :], np.sin(ang), 0.0)
    hi = (e >= half)[None, :]
    return (jnp.asarray(cos, F32), jnp.asarray(np.where(hi, sin, 0.0), F32),
            jnp.asarray(np.where(hi, 0.0, -sin), F32))


def _pair_perm():
    n = A_HEADS // A_KV_HEADS
    idx = []
    for j in range(n):
        idx += list(range(j * A_HEAD_DIM, (j + 1) * A_HEAD_DIM))
        idx += list(range((n + j) * A_HEAD_DIM, (n + j + 1) * A_HEAD_DIM))
    return np.asarray(idx)


def kernel(x, c, ctx, c_ctx, norm_g, w_ada, b_ada, w_in0, sink0, gm_ln_g, gm_ln_b, gm_ws, gm_bs, w_out0, w_in1,
           q_norm, w_qb, kv_norm, w_kvb, w_out1, final_g):
    b, s, d = x.shape
    sx = ctx.shape[1]

    cond = jnp.concatenate([c, c_ctx[None, :], jnp.zeros((16 - b - 1, d), F32)], axis=0)
    mod = _ada_call(cond, w_ada, b_ada)
    parts = lambda l, r0, r1: [mod[l, r0:r1, i * d:(i + 1) * d][:, None, :] for i in range(3)]

    perm = _pair_perm()
    mix_perm = np.concatenate([perm, A_WIDTH + np.arange(B_WIDTH)])
    o = A_WIDTH
    w0 = w_in0[0]
    wts0 = {
        "wq": w0[:, :o][:, perm].astype(BF16),
        "wkv": w0[:, o:o + 2 * A_KV_WIDTH].astype(BF16),
        "wu": w0[:, o + 2 * A_KV_WIDTH:o + 2 * A_KV_WIDTH + B_WIDTH].astype(BF16),
        "wvv": w0[:, o + 2 * A_KV_WIDTH + B_WIDTH:o + 2 * A_KV_WIDTH + 2 * B_WIDTH].astype(BF16),
        "wg": w0[:, o + 2 * A_KV_WIDTH + 2 * B_WIDTH:][:, mix_perm].astype(BF16),
        "lng": gm_ln_g[0][None, :], "lnb": gm_ln_b[0][None, :],
    }
    tabs0 = _rope_tables(s, 0, A_HEAD_DIM // 2, A_HEAD_DIM // 4)
    g0 = norm_g[0][None, :]
    sh_l, sc_l, gt_l = parts(0, 0, b)
    sh_c, sc_c, gt_c = parts(0, b, b + 1)
    q, k, v, gu, vn, sg = _in0_call(x, g0, sc_l, sh_l, wts0, tabs0, 512, True)
    qc, kc, vc, guc, vnc, sgc = _in0_call(ctx, g0, sc_c, sh_c, wts0, tabs0, sx, False)

    ws = gm_ws[0].astype(BF16)
    bs = jnp.repeat(gm_bs[0].T, B_GROUP_DIM, axis=1)
    wo0 = w_out0[0][mix_perm].astype(BF16)
    sink = sink0[0]
    x1 = _mix0_call(sink, q, k, v, kc, vc, gu, vn, sg, x, gt_l, ws, bs, wo0, True)
    xc1 = _mix0_call(sink, qc, None, None, kc, vc, guc, vnc, sgc, ctx, gt_c, ws, bs, wo0, False)

    w1 = w_in1[0]
    kvw = w_kvb[0].reshape(KV_RANK, C_HEADS, C_NOPE + C_V)
    wk = jnp.zeros((2 * LANES, C_HEADS, LANES), F32)
    wk = wk.at[:KV_RANK, :, :C_NOPE].set(kvw[:, :, :C_NOPE])
    eye = jnp.broadcast_to(jnp.eye(C_ROPE, dtype=F32)[:, None, :], (C_ROPE, C_HEADS, C_ROPE))
    wk = wk.at[KV_RANK + C_NOPE:KV_RANK + C_NOPE + C_ROPE, :, C_NOPE:C_NOPE + C_ROPE].set(eye)
    wkv = jnp.zeros((d, 2 * LANES), F32)
    wkv = wkv.at[:, :KV_RANK].set(w1[:, Q_RANK:Q_RANK + KV_RANK])
    wkv = wkv.at[:, KV_RANK + C_NOPE:KV_RANK + C_NOPE + C_ROPE].set(
        w1[:, Q_RANK + KV_RANK:Q_RANK + KV_RANK + C_ROPE])
    wqb = jnp.pad(w_qb[0].reshape(Q_RANK, C_HEADS, C_NOPE + C_ROPE),
                  ((0, 0), (0, 0), (0, LANES - C_NOPE - C_ROPE)))
    wts1 = {
        "wqa": w1[:, :Q_RANK].astype(BF16), "qn": q_norm[0][None, :],
        "wqb": wqb.reshape(Q_RANK, C_HEADS * LANES).astype(BF16),
        "wkv": wkv.astype(BF16), "kvn": kv_norm[0][None, :],
        "wk": wk.reshape(2 * LANES, C_HEADS * LANES).astype(BF16),
        "wvt": kvw[:, :, C_NOPE:].reshape(KV_RANK, C_WIDTH).T.astype(BF16),
        "wg": w1[:, Q_RANK + KV_RANK + C_ROPE:].astype(BF16),
    }
    tabs1 = _rope_tables(s, C_NOPE, C_ROPE // 2, C_ROPE // 4)
    g1 = norm_g[1][None, :]
    sh_l, sc_l, gt_l = parts(1, 0, b)
    sh_c, sc_c, _ = parts(1, b, b + 1)
    q1, k1, vt1, sg1 = _in1_call(x1, g1, sc_l, sh_l, wts1, tabs1, 512, True)
    kx1, vxt1 = _in1_call(xc1, g1, sc_c, sh_c, wts1, None, sx, False)
    og = _attn1_call(q1, k1, kx1, vt1, vxt1, sg1, 256)
    return _out1_call(og, x1, gt_l, w_out1[0].astype(BF16), final_g[None, :], 512)
```

```python
import functools

import numpy as np
import jax
import jax.numpy as jnp
from jax import lax
from jax.experimental import pallas as pl
from jax.experimental.pallas import tpu as pltpu

D_MODEL = 1024
GRID_W = 64
ROPE_BASE = 10000.0
NORM_EPS = 1e-6
NEG_INF = -1e30

A_HEADS = 8
A_KV_HEADS = 2
A_HEAD_DIM = 64
A_WIDTH = A_HEADS * A_HEAD_DIM
A_KV_WIDTH = A_KV_HEADS * A_HEAD_DIM
A_BLOCK = 128
B_GROUPS = 8
B_WIDTH = 512
B_GROUP_DIM = 64
EVEN_MIX = A_WIDTH + B_WIDTH

C_HEADS = 16
C_NOPE = 64
C_ROPE = 32
C_V = 64
Q_RANK = 256
KV_RANK = 128
C_WIDTH = C_HEADS * C_V

LANES = 128
VMEM_LIMIT = 48 * 1024 * 1024

LOG2_E = 1.4426950408889634

BF16 = jnp.bfloat16
F32 = jnp.float32


def _params(n_axes):
    return pltpu.CompilerParams(dimension_semantics=("parallel",) * n_axes,
                                vmem_limit_bytes=VMEM_LIMIT)


def _dot(a, b):
    return jnp.dot(a, b, preferred_element_type=F32)


def _dot_nt(a, b):
    return lax.dot_general(a, b, (((1,), (1,)), ((), ())), preferred_element_type=F32)


def _rope(x, cos, sin_hi, sin_lo, half):
    return (x * cos + pltpu.roll(x, half, 1) * sin_hi
            + pltpu.roll(x, LANES - half, 1) * sin_lo)


def _gelu(x):
    return 0.5 * x * (1.0 + lax.erf(x * (2.0 ** -0.5)))


def _modulated_rms(x, g, scale, shift):
    r = lax.rsqrt(jnp.mean(x * x, axis=-1, keepdims=True) + NORM_EPS)
    return (x * r) * (g * (1.0 + scale)) + shift


def _ada_body(c_ref, w_ref, b_ref, o_ref):
    c = c_ref[...]
    cs = c * jax.nn.sigmoid(c)
    o_ref[0] = _dot(cs, w_ref[0]) + b_ref[0]


def _ada_call(cond, w_ada, b_ada):
    depth, d, n3 = w_ada.shape
    rows = cond.shape[0]
    tn = 768
    return pl.pallas_call(
        _ada_body,
        grid=(depth, n3 // tn),
        in_specs=[pl.BlockSpec((rows, d), lambda l, j: (0, 0)),
                  pl.BlockSpec((1, d, tn), lambda l, j: (l, 0, j)),
                  pl.BlockSpec((1, 1, tn), lambda l, j: (l, 0, j))],
        out_specs=pl.BlockSpec((1, rows, tn), lambda l, j: (l, 0, j)),
        out_shape=jax.ShapeDtypeStruct((depth, rows, n3), F32),
        compiler_params=_params(2),
        name="ada",
    )(cond, w_ada, b_ada.reshape(depth, 1, n3))


def _in0_body(use_rope, x_ref, g_ref, sc_ref, sh_ref, wq_ref, wk_ref, wvt_ref, wu_ref, wvv_ref, wg_ref,
              lng_ref, lnb_ref, cos_ref, shi_ref, slo_ref,
              q_ref, k_ref, vt_ref, gu_ref, vn_ref, sg_ref):
    h = _modulated_rms(x_ref[0], g_ref[...], sc_ref[0], sh_ref[0]).astype(BF16)
    if use_rope:
        cos, shi, slo = cos_ref[...], shi_ref[...], slo_ref[...]
        rope = lambda t: _rope(t, cos, shi, slo, 16)
    else:
        rope = lambda t: t
    scale = A_HEAD_DIM ** -0.5 * LOG2_E
    q = _dot(h, wq_ref[...])
    for j in range(A_WIDTH // LANES):
        sl = slice(j * LANES, (j + 1) * LANES)
        q_ref[0, :, sl] = (rope(q[:, sl]) * scale).astype(BF16)
    k_ref[0] = rope(_dot(h, wk_ref[...])).astype(BF16)
    vt_ref[0] = _dot_nt(wvt_ref[...], h).astype(BF16)
    gu_ref[0] = _gelu(_dot(h, wu_ref[...])).astype(BF16)
    gv = _gelu(_dot(h, wvv_ref[...]))
    mu = jnp.mean(gv, axis=-1, keepdims=True)
    gc = gv - mu
    vn = gc * lax.rsqrt(jnp.mean(gc * gc, axis=-1, keepdims=True) + NORM_EPS)
    vn_ref[0] = (vn * lng_ref[...] + lnb_ref[...]).astype(BF16)
    gate = _dot(h, wg_ref[...])
    sg_ref[0] = (gate * jax.nn.sigmoid(gate)).astype(BF16)


def _in0_call(x, g, scale, shift, w, tabs, tm, use_rope):
    b, s, d = x.shape
    nmod = scale.shape[0]
    mod_map = (lambda i, j: (i, 0, 0)) if nmod > 1 else (lambda i, j: (0, 0, 0))
    full = lambda a: pl.BlockSpec(a.shape, lambda i, j: (0,) * a.ndim)
    row = lambda n: pl.BlockSpec((1, tm, n), lambda i, j: (i, j, 0))
    tab = pl.BlockSpec((tm, LANES), lambda i, j: (j, 0))
    widths = [A_WIDTH, A_KV_WIDTH, None, B_WIDTH, B_WIDTH, EVEN_MIX]
    vt_spec = pl.BlockSpec((1, A_KV_WIDTH, tm), lambda i, j: (i, 0, j))
    vt_shape = jax.ShapeDtypeStruct((b, A_KV_WIDTH, s), BF16)
    return pl.pallas_call(
        functools.partial(_in0_body, use_rope),
        grid=(b, s // tm),
        in_specs=[row(d), full(g), pl.BlockSpec((1, 1, d), mod_map), pl.BlockSpec((1, 1, d), mod_map),
                  full(w["wq"]), full(w["wk"]), full(w["wvt"]), full(w["wu"]), full(w["wvv"]), full(w["wg"]),
                  full(w["lng"]), full(w["lnb"]), tab, tab, tab],
        out_specs=[vt_spec if n is None else row(n) for n in widths],
        out_shape=[vt_shape if n is None else jax.ShapeDtypeStruct((b, s, n), BF16) for n in widths],
        compiler_params=_params(2),
        name="in0_lat" if use_rope else "in0_ctx",
    )(x, g, scale, shift, w["wq"], w["wk"], w["wvt"], w["wu"], w["wvv"], w["wg"], w["lng"], w["lnb"], *tabs)


def _mix0_body(has_local, nchunk, sink_ref, q_ref, *refs):
    if has_local:
        kp_ref, kc_ref, kn_ref, vp_ref, vc_ref, vn_ref, *refs = refs
    kx_ref, vx_ref, gu_ref, vnrm_ref, sg_ref, x_ref, gl_ref, ws_ref, bs_ref, wo_ref, o_ref = refs
    rows = q_ref.shape[1]
    npair = A_WIDTH // LANES
    ncol = npair * rows
    nx = kx_ref.shape[1]
    lane_lo = lax.broadcasted_iota(jnp.int32, (rows, LANES), 1) < A_HEAD_DIM
    chunk_lane_lo = lax.broadcasted_iota(jnp.int32, (A_BLOCK, LANES), 1) < A_HEAD_DIM
    if has_local:
        i = pl.program_id(1)
        nblk = pl.num_programs(1)
        key = lax.broadcasted_iota(jnp.int32, (A_BLOCK, ncol), 0)
        qry = lax.broadcasted_iota(jnp.int32, (A_BLOCK, ncol), 1) & (rows - 1)
        mask_prev = (key >= qry) & (i > 0)
        mask_next = (key <= qry) & (i < nblk - 1)
        k_all = jnp.concatenate([kx_ref[0], kp_ref[0], kc_ref[0], kn_ref[0]], axis=0)
        vt_all = jnp.concatenate([vx_ref[0], vp_ref[0], vc_ref[0], vn_ref[0]], axis=1)
    else:
        k_all = kx_ref[0]
        vt_all = vx_ref[0]

    q_pairs = [q_ref[0, :, jp * LANES:(jp + 1) * LANES] for jp in range(npair)]
    out_t = []
    for grp in range(A_KV_HEADS):
        keep = lane_lo if grp == 0 else ~lane_lo
        qm = jnp.concatenate([jnp.where(keep, qp, jnp.zeros_like(qp)) for qp in q_pairs], axis=0)
        s_t = _dot_nt(k_all, qm)
        if has_local:
            parts = [s_t[:nx],
                     jnp.where(mask_prev, s_t[nx:nx + A_BLOCK], NEG_INF),
                     s_t[nx + A_BLOCK:nx + 2 * A_BLOCK],
                     jnp.where(mask_next, s_t[nx + 2 * A_BLOCK:], NEG_INF)]
        else:
            parts = [s_t]
        sink = sink_ref[grp:grp + 1, :] * LOG2_E
        m = sink
        for part in parts:
            m = jnp.maximum(m, jnp.max(part, axis=0, keepdims=True))
        den = jnp.exp2(sink - m)
        probs = []
        for part in parts:
            p = jnp.exp2(part - m)
            den = den + jnp.sum(p, axis=0, keepdims=True)
            probs.append(p.astype(BF16))
        p_t = probs[0] if len(probs) == 1 else jnp.concatenate(probs, axis=0)
        out_t.append(_dot(vt_all, p_t) / den)

    row_lo = lax.broadcasted_iota(jnp.int32, (LANES, rows), 0) < A_HEAD_DIM
    pieces = []
    for jp in range(npair):
        cs = slice(jp * rows, (jp + 1) * rows)
        pieces.append(jnp.where(row_lo, out_t[0][:, cs], out_t[1][:, cs]).T)

    for jp in range(B_WIDTH // LANES):
        sl = slice(jp * LANES, (jp + 1) * LANES)
        chunks = []
        for ch in range(nchunk):
            rs = slice(ch * A_BLOCK, (ch + 1) * A_BLOCK)
            vp = vnrm_ref[0, rs, sl]
            s0 = _dot(ws_ref[2 * jp], vp)
            s1 = _dot(ws_ref[2 * jp + 1], vp)
            chunks.append(jnp.where(chunk_lane_lo, s0, s1) + bs_ref[:, sl])
        sg = chunks[0] if nchunk == 1 else jnp.concatenate(chunks, axis=0)
        pieces.append(gu_ref[0, :, sl].astype(F32) * sg)

    mixed = jnp.concatenate(pieces, axis=-1) * sg_ref[0].astype(F32)
    y = _dot(mixed.astype(BF16), wo_ref[...])
    o_ref[0] = x_ref[0] + gl_ref[0] * y


def _mix0_call(sink, q, k, vt, kx, vxt, gu, vnrm, sg, x, gate, ws, bs, wo, has_local):
    b, s, d = x.shape
    rows = A_BLOCK if has_local else s
    nchunk = rows // A_BLOCK
    nblk = s // rows
    nmod = gate.shape[0]
    mod_map = (lambda i, j: (i, 0, 0)) if nmod > 1 else (lambda i, j: (0, 0, 0))
    full = lambda a: pl.BlockSpec(a.shape, lambda i, j: (0,) * a.ndim)
    row = lambda n: pl.BlockSpec((1, rows, n), lambda i, j: (i, j, 0))
    ctx = lambda a: pl.BlockSpec((1,) + a.shape[1:], lambda i, j: (i, 0, 0))
    sink_rows = jnp.repeat(sink.reshape(A_KV_HEADS, A_HEADS // A_KV_HEADS), rows, axis=1)
    in_specs = [full(sink_rows), row(A_WIDTH)]
    args = [sink_rows, q]
    if has_local:
        prev = lambda j: jnp.maximum(j - 1, 0)
        nxt = lambda j: jnp.minimum(j + 1, nblk - 1)
        kblk = lambda f: pl.BlockSpec((1, A_BLOCK, A_KV_WIDTH), lambda i, j: (i, f(j), 0))
        vblk = lambda f: pl.BlockSpec((1, A_KV_WIDTH, A_BLOCK), lambda i, j: (i, 0, f(j)))
        in_specs += [kblk(prev), kblk(lambda j: j), kblk(nxt), vblk(prev), vblk(lambda j: j), vblk(nxt)]
        args += [k, k, k, vt, vt, vt]
    in_specs += [ctx(kx), ctx(vxt), row(B_WIDTH), row(B_WIDTH), row(EVEN_MIX), row(d),
                 pl.BlockSpec((1, 1, d), mod_map), full(ws), full(bs), full(wo)]
    args += [kx, vxt, gu, vnrm, sg, x, gate, ws, bs, wo]
    return pl.pallas_call(
        functools.partial(_mix0_body, has_local, nchunk),
        grid=(b, nblk),
        in_specs=in_specs,
        out_specs=row(d),
        out_shape=jax.ShapeDtypeStruct((b, s, d), F32),
        compiler_params=_params(2),
        name="mix0_lat" if has_local else "mix0_ctx",
    )(*args)


def _rms(x, g):
    return x * lax.rsqrt(jnp.mean(x * x, axis=-1, keepdims=True) + NORM_EPS) * g


def _in1_kv(h, use_rope, wkv_ref, kvn_ref, wk_ref, wvt_ref, tabs, k_ref, vt_ref):
    kva = _dot(h, wkv_ref[...])
    kvn = _rms(kva[:, :KV_RANK], kvn_ref[...]).astype(BF16)
    kpe = kva[:, KV_RANK:]
    if use_rope:
        kpe = _rope(kpe, *tabs, 8)
    kin = jnp.concatenate([kvn, kpe.astype(BF16)], axis=-1)
    k_ref[0] = _dot(kin, wk_ref[...]).astype(BF16)
    vt_ref[0] = _dot_nt(wvt_ref[...], kvn).astype(BF16)


def _in1_lat_body(x_ref, g_ref, sc_ref, sh_ref, wqa_ref, qn_ref, wqb_ref, wkv_ref, kvn_ref, wk_ref, wvt_ref,
                  wg_ref, cos_ref, shi_ref, slo_ref, q_ref, k_ref, vt_ref, sg_ref):
    h = _modulated_rms(x_ref[0], g_ref[...], sc_ref[0], sh_ref[0]).astype(BF16)
    tabs = (cos_ref[...], shi_ref[...], slo_ref[...])
    qn = _rms(_dot(h, wqa_ref[...]), qn_ref[...]).astype(BF16)
    scale = (C_NOPE + C_ROPE) ** -0.5 * LOG2_E
    for hp in range(C_HEADS // 2):
        q2 = _dot(qn, wqb_ref[:, hp * 2 * LANES:(hp + 1) * 2 * LANES])
        for t in range(2):
            sl = slice(t * LANES, (t + 1) * LANES)
            q_ref[0, :, (2 * hp + t) * LANES:(2 * hp + t + 1) * LANES] = (
                _rope(q2[:, sl], *tabs, 8) * scale).astype(BF16)
    _in1_kv(h, True, wkv_ref, kvn_ref, wk_ref, wvt_ref, tabs, k_ref, vt_ref)
    gate = _dot(h, wg_ref[...])
    sg_ref[0] = (gate * jax.nn.sigmoid(gate)).astype(BF16)


def _in1_ctx_body(x_ref, g_ref, sc_ref, sh_ref, wkv_ref, kvn_ref, wk_ref, wvt_ref, k_ref, vt_ref):
    h = _modulated_rms(x_ref[0], g_ref[...], sc_ref[0], sh_ref[0]).astype(BF16)
    _in1_kv(h, False, wkv_ref, kvn_ref, wk_ref, wvt_ref, None, k_ref, vt_ref)


def _in1_call(x, g, scale, shift, w, tabs, tm, latent):
    b, s, d = x.shape
    nmod = scale.shape[0]
    mod_map = (lambda i, j: (i, 0, 0)) if nmod > 1 else (lambda i, j: (0, 0, 0))
    full = lambda a: pl.BlockSpec(a.shape, lambda i, j: (0,) * a.ndim)
    row = lambda n: pl.BlockSpec((1, tm, n), lambda i, j: (i, j, 0))
    tab = pl.BlockSpec((tm, LANES), lambda i, j: (j, 0))
    mod = pl.BlockSpec((1, 1, d), mod_map)
    kw = C_HEADS * LANES
    k_shape = jax.ShapeDtypeStruct((b, s, kw), BF16)
    vt_shape = jax.ShapeDtypeStruct((b, C_WIDTH, s), BF16)
    vt_spec = pl.BlockSpec((1, C_WIDTH, tm), lambda i, j: (i, 0, j))
    kv_w = [w["wkv"], w["kvn"], w["wk"], w["wvt"]]
    if latent:
        return pl.pallas_call(
            _in1_lat_body,
            grid=(b, s // tm),
            in_specs=[row(d), full(g), mod, mod, full(w["wqa"]), full(w["qn"]), full(w["wqb"])]
                     + [full(a) for a in kv_w] + [full(w["wg"]), tab, tab, tab],
            out_specs=[row(kw), row(kw), vt_spec, row(C_WIDTH)],
            out_shape=[k_shape, k_shape, vt_shape, jax.ShapeDtypeStruct((b, s, C_WIDTH), BF16)],
            compiler_params=_params(2),
            name="in1_lat",
        )(x, g, scale, shift, w["wqa"], w["qn"], w["wqb"], *kv_w, w["wg"], *tabs)
    return pl.pallas_call(
        _in1_ctx_body,
        grid=(b, s // tm),
        in_specs=[row(d), full(g), mod, mod] + [full(a) for a in kv_w],
        out_specs=[row(kw), vt_spec],
        out_shape=[k_shape, vt_shape],
        compiler_params=_params(2),
        name="in1_ctx",
    )(x, g, scale, shift, *kv_w)


ATTN1_KEY_CHUNK = 1024


def _attn1_body(q_ref, k_ref, kx_ref, vt_ref, vxt_ref, sg_ref, o_ref):
    tiles = []
    for kr, vr in ((k_ref, vt_ref), (kx_ref, vxt_ref)):
        n_keys = kr.shape[1]
        tiles += [(kr, vr, lo, min(lo + ATTN1_KEY_CHUNK, n_keys)) for lo in range(0, n_keys, ATTN1_KEY_CHUNK)]
    state = [None, None]

    def scores(n, t):
        kr, _, lo, hi = tiles[n]
        return _dot_nt(kr[0, lo:hi, t * LANES:(t + 1) * LANES],
                       q_ref[0, :, t * LANES:(t + 1) * LANES])

    s_next = [scores(0, t) for t in range(2)]
    for n, (_, vr, lo, hi) in enumerate(tiles):
        for t in range(2):
            s = s_next[t]
            if n + 1 < len(tiles):
                s_next[t] = scores(n + 1, t)
            v_t = vr[0, t * C_V:(t + 1) * C_V, lo:hi]
            m_tile = jnp.max(s, axis=0, keepdims=True)
            if state[t] is None:
                m = m_tile
                p = jnp.exp2(s - m)
                den = jnp.sum(p, axis=0, keepdims=True)
                acc = _dot(v_t, p.astype(BF16))
            else:
                m_old, den_old, acc_old = state[t]
                m = jnp.maximum(m_old, m_tile)
                alpha = jnp.exp2(m_old - m)
                p = jnp.exp2(s - m)
                den = den_old * alpha + jnp.sum(p, axis=0, keepdims=True)
                acc = acc_old * alpha + _dot(v_t, p.astype(BF16))
            state[t] = (m, den, acc)
    o = jnp.concatenate([acc / den for _, den, acc in state], axis=0).T
    o_ref[0] = (o * sg_ref[0].astype(F32)).astype(BF16)


def _attn1_call(q, k, kx, vt, vxt, sg, tq):
    b, s, _ = q.shape
    sx = kx.shape[1]
    npair = C_HEADS // 2
    return pl.pallas_call(
        _attn1_body,
        grid=(b, npair, s // tq),
        in_specs=[pl.BlockSpec((1, tq, 2 * LANES), lambda i, h, j: (i, j, h)),
                  pl.BlockSpec((1, s, 2 * LANES), lambda i, h, j: (i, 0, h)),
                  pl.BlockSpec((1, sx, 2 * LANES), lambda i, h, j: (i, 0, h)),
                  pl.BlockSpec((1, 2 * C_V, s), lambda i, h, j: (i, h, 0)),
                  pl.BlockSpec((1, 2 * C_V, sx), lambda i, h, j: (i, h, 0)),
                  pl.BlockSpec((1, tq, 2 * C_V), lambda i, h, j: (i, j, h))],
        out_specs=pl.BlockSpec((1, tq, 2 * C_V), lambda i, h, j: (i, j, h)),
        out_shape=jax.ShapeDtypeStruct((b, s, C_WIDTH), BF16),
        compiler_params=_params(3),
        name="attn1",
    )(q, k, kx, vt, vxt, sg)


def _out1_body(og_ref, x_ref, gl_ref, wo_ref, fg_ref, o_ref):
    x = x_ref[0] + gl_ref[0] * _dot(og_ref[0], wo_ref[...])
    o_ref[0] = _rms(x, fg_ref[...])


def _out1_call(og, x, gate, wo, fg, tm):
    b, s, d = x.shape
    full = lambda a: pl.BlockSpec(a.shape, lambda i, j: (0,) * a.ndim)
    row = lambda n: pl.BlockSpec((1, tm, n), lambda i, j: (i, j, 0))
    return pl.pallas_call(
        _out1_body,
        grid=(b, s // tm),
        in_specs=[row(C_WIDTH), row(d), pl.BlockSpec((1, 1, d), lambda i, j: (i, 0, 0)), full(wo), full(fg)],
        out_specs=row(d),
        out_shape=jax.ShapeDtypeStruct((b, s, d), F32),
        compiler_params=_params(2),
        name="out1",
    )(og, x, gate, wo, fg)


def _rope_tables(seq, offset, width, half):
    t = np.arange(seq)
    pos = np.stack([t // GRID_W, t % GRID_W], axis=0).astype(np.float32)
    lane = np.arange(LANES)
    d = (lane - offset) % (2 * width) if offset else lane % (2 * width)
    active = np.ones(LANES, bool) if not offset else (lane >= offset) & (lane < offset + 2 * width)
    axis = d // width
    e = d % width
    f = e % half
    inv = (ROPE_BASE ** (-(np.arange(half, dtype=np.float32)) / half)).astype(np.float32)
    ang = pos[axis, :].T * inv[f][None, :]
    cos = np.where(active[None, :], np.cos(ang), 1.0)
    sin = np.where(active[None, :], np.sin(ang), 0.0)
    hi = (e >= half)[None, :]
    return (jnp.asarray(cos, F32), jnp.asarray(np.where(hi, sin, 0.0), F32),
            jnp.asarray(np.where(hi, 0.0, -sin), F32))


def _pair_perm():
    n = A_HEADS // A_KV_HEADS
    idx = []
    for j in range(n):
        idx += list(range(j * A_HEAD_DIM, (j + 1) * A_HEAD_DIM))
        idx += list(range((n + j) * A_HEAD_DIM, (n + j + 1) * A_HEAD_DIM))
    return np.asarray(idx)


def kernel(x, c, ctx, c_ctx, norm_g, w_ada, b_ada, w_in0, sink0, gm_ln_g, gm_ln_b, gm_ws, gm_bs, w_out0, w_in1,
           q_norm, w_qb, kv_norm, w_kvb, w_out1, final_g):
    b, s, d = x.shape
    sx = ctx.shape[1]

    cond = jnp.concatenate([c, c_ctx[None, :], jnp.zeros((16 - b - 1, d), F32)], axis=0)
    mod = _ada_call(cond, w_ada, b_ada)
    parts = lambda l, r0, r1: [mod[l, r0:r1, i * d:(i + 1) * d][:, None, :] for i in range(3)]

    perm = _pair_perm()
    mix_perm = np.concatenate([perm, A_WIDTH + np.arange(B_WIDTH)])
    o = A_WIDTH
    w0 = w_in0[0]
    wts0 = {
        "wq": w0[:, :o][:, perm].astype(BF16),
        "wk": w0[:, o:o + A_KV_WIDTH].astype(BF16),
        "wvt": w0[:, o + A_KV_WIDTH:o + 2 * A_KV_WIDTH].T.astype(BF16),
        "wu": w0[:, o + 2 * A_KV_WIDTH:o + 2 * A_KV_WIDTH + B_WIDTH].astype(BF16),
        "wvv": w0[:, o + 2 * A_KV_WIDTH + B_WIDTH:o + 2 * A_KV_WIDTH + 2 * B_WIDTH].astype(BF16),
        "wg": w0[:, o + 2 * A_KV_WIDTH + 2 * B_WIDTH:][:, mix_perm].astype(BF16),
        "lng": gm_ln_g[0][None, :], "lnb": gm_ln_b[0][None, :],
    }
    tabs0 = _rope_tables(s, 0, A_HEAD_DIM // 2, A_HEAD_DIM // 4)
    g0 = norm_g[0][None, :]
    sh_l, sc_l, gt_l = parts(0, 0, b)
    sh_c, sc_c, gt_c = parts(0, b, b + 1)
    q, k, v, gu, vn, sg = _in0_call(x, g0, sc_l, sh_l, wts0, tabs0, 512, True)
    qc, kc, vc, guc, vnc, sgc = _in0_call(ctx, g0, sc_c, sh_c, wts0, tabs0, sx, False)

    ws = gm_ws[0].astype(BF16)
    bs = jnp.repeat(gm_bs[0].T, B_GROUP_DIM, axis=1)
    wo0 = w_out0[0][mix_perm].astype(BF16)
    sink = sink0[0]
    x1 = _mix0_call(sink, q, k, v, kc, vc, gu, vn, sg, x, gt_l, ws, bs, wo0, True)
    xc1 = _mix0_call(sink, qc, None, None, kc, vc, guc, vnc, sgc, ctx, gt_c, ws, bs, wo0, False)

    w1 = w_in1[0]
    kvw = w_kvb[0].reshape(KV_RANK, C_HEADS, C_NOPE + C_V)
    wk = jnp.zeros((2 * LANES, C_HEADS, LANES), F32)
    wk = wk.at[:KV_RANK, :, :C_NOPE].set(kvw[:, :, :C_NOPE])
    eye = jnp.broadcast_to(jnp.eye(C_ROPE, dtype=F32)[:, None, :], (C_ROPE, C_HEADS, C_ROPE))
    wk = wk.at[KV_RANK + C_NOPE:KV_RANK + C_NOPE + C_ROPE, :, C_NOPE:C_NOPE + C_ROPE].set(eye)
    wkv = jnp.zeros((d, 2 * LANES), F32)
    wkv = wkv.at[:, :KV_RANK].set(w1[:, Q_RANK:Q_RANK + KV_RANK])
    wkv = wkv.at[:, KV_RANK + C_NOPE:KV_RANK + C_NOPE + C_ROPE].set(
        w1[:, Q_RANK + KV_RANK:Q_RANK + KV_RANK + C_ROPE])
    wqb = jnp.pad(w_qb[0].reshape(Q_RANK, C_HEADS, C_NOPE + C_ROPE),
                  ((0, 0), (0, 0), (0, LANES - C_NOPE - C_ROPE)))
    wts1 = {
        "wqa": w1[:, :Q_RANK].astype(BF16), "qn": q_norm[0][None, :],
        "wqb": wqb.reshape(Q_RANK, C_HEADS * LANES).astype(BF16),
        "wkv": wkv.astype(BF16), "kvn": kv_norm[0][None, :],
        "wk": wk.reshape(2 * LANES, C_HEADS * LANES).astype(BF16),
        "wvt": kvw[:, :, C_NOPE:].reshape(KV_RANK, C_WIDTH).T.astype(BF16),
        "wg": w1[:, Q_RANK + KV_RANK + C_ROPE:].astype(BF16),
    }
    tabs1 = _rope_tables(s, C_NOPE, C_ROPE // 2, C_ROPE // 4)
    g1 = norm_g[1][None, :]
    sh_l, sc_l, gt_l = parts(1, 0, b)
    sh_c, sc_c, _ = parts(1, b, b + 1)
    q1, k1, vt1, sg1 = _in1_call(x1, g1, sc_l, sh_l, wts1, tabs1, 512, True)
    kx1, vxt1 = _in1_call(xc1, g1, sc_c, sh_c, wts1, None, sx, False)
    og = _attn1_call(q1, k1, kx1, vt1, vxt1, sg1, 256)
    return _out1_call(og, x1, gt_l, w_out1[0].astype(BF16), final_g[None, :], 512)
```

```python
import functools

import numpy as np
import jax
import jax.numpy as jnp
from jax import lax
from jax.experimental import pallas as pl
from jax.experimental.pallas import tpu as pltpu

D_MODEL = 1024
GRID_W = 64
ROPE_BASE = 10000.0
NORM_EPS = 1e-6
NEG_INF = -1e30

A_HEADS = 8
A_KV_HEADS = 2
A_HEAD_DIM = 64
A_WIDTH = A_HEADS * A_HEAD_DIM
A_KV_WIDTH = A_KV_HEADS * A_HEAD_DIM
A_BLOCK = 128
B_GROUPS = 8
B_WIDTH = 512
B_GROUP_DIM = 64
EVEN_MIX = A_WIDTH + B_WIDTH

C_HEADS = 16
C_NOPE = 64
C_ROPE = 32
C_V = 64
Q_RANK = 256
KV_RANK = 128
C_WIDTH = C_HEADS * C_V

LANES = 128
VMEM_LIMIT = 48 * 1024 * 1024

LOG2_E = 1.4426950408889634

BF16 = jnp.bfloat16
F32 = jnp.float32


def _params(n_axes):
    return pltpu.CompilerParams(dimension_semantics=("parallel",) * n_axes,
                                vmem_limit_bytes=VMEM_LIMIT)


def _dot(a, b):
    return jnp.dot(a, b, preferred_element_type=F32)


def _dot_nt(a, b):
    return lax.dot_general(a, b, (((1,), (1,)), ((), ())), preferred_element_type=F32)


def _rope(x, cos, sin_hi, sin_lo, half):
    return (x * cos + pltpu.roll(x, half, 1) * sin_hi
            + pltpu.roll(x, LANES - half, 1) * sin_lo)


def _gelu(x):
    return 0.5 * x * (1.0 + lax.erf(x * (2.0 ** -0.5)))


def _modulated_rms(x, g, scale, shift):
    r = lax.rsqrt(jnp.mean(x * x, axis=-1, keepdims=True) + NORM_EPS)
    return (x * r) * (g * (1.0 + scale)) + shift


def _ada_body(c_ref, w_ref, b_ref, o_ref):
    c = c_ref[...]
    cs = c * jax.nn.sigmoid(c)
    o_ref[0] = _dot(cs, w_ref[0]) + b_ref[0]


def _ada_call(cond, w_ada, b_ada):
    depth, d, n3 = w_ada.shape
    rows = cond.shape[0]
    tn = 768
    return pl.pallas_call(
        _ada_body,
        grid=(depth, n3 // tn),
        in_specs=[pl.BlockSpec((rows, d), lambda l, j: (0, 0)),
                  pl.BlockSpec((1, d, tn), lambda l, j: (l, 0, j)),
                  pl.BlockSpec((1, 1, tn), lambda l, j: (l, 0, j))],
        out_specs=pl.BlockSpec((1, rows, tn), lambda l, j: (l, 0, j)),
        out_shape=jax.ShapeDtypeStruct((depth, rows, n3), F32),
        compiler_params=_params(2),
        name="ada",
    )(cond, w_ada, b_ada.reshape(depth, 1, n3))


def _in0_body(use_rope, x_ref, g_ref, sc_ref, sh_ref, wq_ref, wk_ref, wvt_ref, wu_ref, wvv_ref, wg_ref,
              lng_ref, lnb_ref, cos_ref, shi_ref, slo_ref,
              q_ref, k_ref, vt_ref, gu_ref, vn_ref, sg_ref):
    h = _modulated_rms(x_ref[0], g_ref[...], sc_ref[0], sh_ref[0]).astype(BF16)
    if use_rope:
        cos, shi, slo = cos_ref[...], shi_ref[...], slo_ref[...]
        rope = lambda t: _rope(t, cos, shi, slo, 16)
    else:
        rope = lambda t: t
    scale = A_HEAD_DIM ** -0.5 * LOG2_E
    q = _dot(h, wq_ref[...])
    for j in range(A_WIDTH // LANES):
        sl = slice(j * LANES, (j + 1) * LANES)
        q_ref[0, :, sl] = (rope(q[:, sl]) * scale).astype(BF16)
    k_ref[0] = rope(_dot(h, wk_ref[...])).astype(BF16)
    vt_ref[0] = _dot_nt(wvt_ref[...], h).astype(BF16)
    gu_ref[0] = _gelu(_dot(h, wu_ref[...])).astype(BF16)
    gv = _gelu(_dot(h, wvv_ref[...]))
    mu = jnp.mean(gv, axis=-1, keepdims=True)
    gc = gv - mu
    vn = gc * lax.rsqrt(jnp.mean(gc * gc, axis=-1, keepdims=True) + NORM_EPS)
    vn_ref[0] = (vn * lng_ref[...] + lnb_ref[...]).astype(BF16)
    gate = _dot(h, wg_ref[...])
    sg_ref[0] = (gate * jax.nn.sigmoid(gate)).astype(BF16)


def _in0_call(x, g, scale, shift, w, tabs, tm, use_rope):
    b, s, d = x.shape
    nmod = scale.shape[0]
    mod_map = (lambda i, j: (i, 0, 0)) if nmod > 1 else (lambda i, j: (0, 0, 0))
    full = lambda a: pl.BlockSpec(a.shape, lambda i, j: (0,) * a.ndim)
    row = lambda n: pl.BlockSpec((1, tm, n), lambda i, j: (i, j, 0))
    tab = pl.BlockSpec((tm, LANES), lambda i, j: (j, 0))
    widths = [A_WIDTH, A_KV_WIDTH, None, B_WIDTH, B_WIDTH, EVEN_MIX]
    vt_spec = pl.BlockSpec((1, A_KV_WIDTH, tm), lambda i, j: (i, 0, j))
    vt_shape = jax.ShapeDtypeStruct((b, A_KV_WIDTH, s), BF16)
    return pl.pallas_call(
        functools.partial(_in0_body, use_rope),
        grid=(b, s // tm),
        in_specs=[row(d), full(g), pl.BlockSpec((1, 1, d), mod_map), pl.BlockSpec((1, 1, d), mod_map),
                  full(w["wq"]), full(w["wk"]), full(w["wvt"]), full(w["wu"]), full(w["wvv"]), full(w["wg"]),
                  full(w["lng"]), full(w["lnb"]), tab, tab, tab],
        out_specs=[vt_spec if n is None else row(n) for n in widths],
        out_shape=[vt_shape if n is None else jax.ShapeDtypeStruct((b, s, n), BF16) for n in widths],
        compiler_params=_params(2),
        name="in0_lat" if use_rope else "in0_ctx",
    )(x, g, scale, shift, w["wq"], w["wk"], w["wvt"], w["wu"], w["wvv"], w["wg"], w["lng"], w["lnb"], *tabs)


def _mix0_body(has_local, nchunk, sink_ref, q_ref, *refs):
    if has_local:
        kp_ref, kc_ref, kn_ref, vp_ref, vc_ref, vn_ref, *refs = refs
    kx_ref, vx_ref, gu_ref, vnrm_ref, sg_ref, x_ref, gl_ref, ws_ref, bs_ref, wo_ref, o_ref = refs
    rows = q_ref.shape[1]
    npair = A_WIDTH // LANES
    ncol = npair * rows
    nx = kx_ref.shape[1]
    lane_lo = lax.broadcasted_iota(jnp.int32, (rows, LANES), 1) < A_HEAD_DIM
    chunk_lane_lo = lax.broadcasted_iota(jnp.int32, (A_BLOCK, LANES), 1) < A_HEAD_DIM
    if has_local:
        i = pl.program_id(1)
        nblk = pl.num_programs(1)
        key = lax.broadcasted_iota(jnp.int32, (A_BLOCK, ncol), 0)
        qry = lax.broadcasted_iota(jnp.int32, (A_BLOCK, ncol), 1) & (rows - 1)
        mask_prev = (key >= qry) & (i > 0)
        mask_next = (key <= qry) & (i < nblk - 1)
        k_all = jnp.concatenate([kx_ref[0], kp_ref[0], kc_ref[0], kn_ref[0]], axis=0)
        vt_all = jnp.concatenate([vx_ref[0], vp_ref[0], vc_ref[0], vn_ref[0]], axis=1)
    else:
        k_all = kx_ref[0]
        vt_all = vx_ref[0]

    q_pairs = [q_ref[0, :, jp * LANES:(jp + 1) * LANES] for jp in range(npair)]
    out_t = []
    for grp in range(A_KV_HEADS):
        keep = lane_lo if grp == 0 else ~lane_lo
        qm = jnp.concatenate([jnp.where(keep, qp, jnp.zeros_like(qp)) for qp in q_pairs], axis=0)
        s_t = _dot_nt(k_all, qm)
        if has_local:
            parts = [s_t[:nx],
                     jnp.where(mask_prev, s_t[nx:nx + A_BLOCK], NEG_INF),
                     s_t[nx + A_BLOCK:nx + 2 * A_BLOCK],
                     jnp.where(mask_next, s_t[nx + 2 * A_BLOCK:], NEG_INF)]
        else:
            parts = [s_t]
        sink = sink_ref[grp:grp + 1, :] * LOG2_E
        m = sink
        for part in parts:
            m = jnp.maximum(m, jnp.max(part, axis=0, keepdims=True))
        den = jnp.exp2(sink - m)
        probs = []
        for part in parts:
            p = jnp.exp2(part - m)
            den = den + jnp.sum(p, axis=0, keepdims=True)
            probs.append(p.astype(BF16))
        p_t = probs[0] if len(probs) == 1 else jnp.concatenate(probs, axis=0)
        out_t.append(_dot(vt_all, p_t) / den)

    row_lo = lax.broadcasted_iota(jnp.int32, (LANES, rows), 0) < A_HEAD_DIM
    pieces = []
    for jp in range(npair):
        cs = slice(jp * rows, (jp + 1) * rows)
        pieces.append(jnp.where(row_lo, out_t[0][:, cs], out_t[1][:, cs]).T)

    for jp in range(B_WIDTH // LANES):
        sl = slice(jp * LANES, (jp + 1) * LANES)
        chunks = []
        for ch in range(nchunk):
            rs = slice(ch * A_BLOCK, (ch + 1) * A_BLOCK)
            vp = vnrm_ref[0, rs, sl]
            s0 = _dot(ws_ref[2 * jp], vp)
            s1 = _dot(ws_ref[2 * jp + 1], vp)
            chunks.append(jnp.where(chunk_lane_lo, s0, s1) + bs_ref[:, sl])
        sg = chunks[0] if nchunk == 1 else jnp.concatenate(chunks, axis=0)
        pieces.append(gu_ref[0, :, sl].astype(F32) * sg)

    mixed = jnp.concatenate(pieces, axis=-1) * sg_ref[0].astype(F32)
    y = _dot(mixed.astype(BF16), wo_ref[...])
    o_ref[0] = x_ref[0] + gl_ref[0] * y


def _mix0_call(sink, q, k, vt, kx, vxt, gu, vnrm, sg, x, gate, ws, bs, wo, has_local):
    b, s, d = x.shape
    rows = A_BLOCK if has_local else s
    nchunk = rows // A_BLOCK
    nblk = s // rows
    nmod = gate.shape[0]
    mod_map = (lambda i, j: (i, 0, 0)) if nmod > 1 else (lambda i, j: (0, 0, 0))
    full = lambda a: pl.BlockSpec(a.shape, lambda i, j: (0,) * a.ndim)
    row = lambda n: pl.BlockSpec((1, rows, n), lambda i, j: (i, j, 0))
    ctx = lambda a: pl.BlockSpec((1,) + a.shape[1:], lambda i, j: (i, 0, 0))
    sink_rows = jnp.repeat(sink.reshape(A_KV_HEADS, A_HEADS // A_KV_HEADS), rows, axis=1)
    in_specs = [full(sink_rows), row(A_WIDTH)]
    args = [sink_rows, q]
    if has_local:
        prev = lambda j: jnp.maximum(j - 1, 0)
        nxt = lambda j: jnp.minimum(j + 1, nblk - 1)
        kblk = lambda f: pl.BlockSpec((1, A_BLOCK, A_KV_WIDTH), lambda i, j: (i, f(j), 0))
        vblk = lambda f: pl.BlockSpec((1, A_KV_WIDTH, A_BLOCK), lambda i, j: (i, 0, f(j)))
        in_specs += [kblk(prev), kblk(lambda j: j), kblk(nxt), vblk(prev), vblk(lambda j: j), vblk(nxt)]
        args += [k, k, k, vt, vt, vt]
    in_specs += [ctx(kx), ctx(vxt), row(B_WIDTH), row(B_WIDTH), row(EVEN_MIX), row(d),
                 pl.BlockSpec((1, 1, d), mod_map), full(ws), full(bs), full(wo)]
    args += [kx, vxt, gu, vnrm, sg, x, gate, ws, bs, wo]
    return pl.pallas_call(
        functools.partial(_mix0_body, has_local, nchunk),
        grid=(b, nblk),
        in_specs=in_specs,
        out_specs=row(d),
        out_shape=jax.ShapeDtypeStruct((b, s, d), F32),
        compiler_params=_params(2),
        name="mix0_lat" if has_local else "mix0_ctx",
    )(*args)


def _rms(x, g):
    return x * lax.rsqrt(jnp.mean(x * x, axis=-1, keepdims=True) + NORM_EPS) * g


def _in1_kv(h, use_rope, wkv_ref, kvn_ref, wk_ref, wvt_ref, tabs, k_ref, vt_ref):
    kva = _dot(h, wkv_ref[...])
    kvn = _rms(kva[:, :KV_RANK], kvn_ref[...]).astype(BF16)
    kpe = kva[:, KV_RANK:]
    if use_rope:
        kpe = _rope(kpe, *tabs, 8)
    kin = jnp.concatenate([kvn, kpe.astype(BF16)], axis=-1)
    k_ref[0] = _dot(kin, wk_ref[...]).astype(BF16)
    vt_ref[0] = _dot_nt(wvt_ref[...], kvn).astype(BF16)


def _in1_lat_body(x_ref, g_ref, sc_ref, sh_ref, wqa_ref, qn_ref, wqb_ref, wkv_ref, kvn_ref, wk_ref, wvt_ref,
                  wg_ref, cos_ref, shi_ref, slo_ref, q_ref, k_ref, vt_ref, sg_ref):
    h = _modulated_rms(x_ref[0], g_ref[...], sc_ref[0], sh_ref[0]).astype(BF16)
    tabs = (cos_ref[...], shi_ref[...], slo_ref[...])
    qn = _rms(_dot(h, wqa_ref[...]), qn_ref[...]).astype(BF16)
    scale = (C_NOPE + C_ROPE) ** -0.5 * LOG2_E
    for hp in range(C_HEADS // 2):
        q2 = _dot(qn, wqb_ref[:, hp * 2 * LANES:(hp + 1) * 2 * LANES])
        for t in range(2):
            sl = slice(t * LANES, (t + 1) * LANES)
            q_ref[0, :, (2 * hp + t) * LANES:(2 * hp + t + 1) * LANES] = (
                _rope(q2[:, sl], *tabs, 8) * scale).astype(BF16)
    _in1_kv(h, True, wkv_ref, kvn_ref, wk_ref, wvt_ref, tabs, k_ref, vt_ref)
    gate = _dot(h, wg_ref[...])
    sg_ref[0] = (gate * jax.nn.sigmoid(gate)).astype(BF16)


def _in1_ctx_body(x_ref, g_ref, sc_ref, sh_ref, wkv_ref, kvn_ref, wk_ref, wvt_ref, k_ref, vt_ref):
    h = _modulated_rms(x_ref[0], g_ref[...], sc_ref[0], sh_ref[0]).astype(BF16)
    _in1_kv(h, False, wkv_ref, kvn_ref, wk_ref, wvt_ref, None, k_ref, vt_ref)


def _in1_call(x, g, scale, shift, w, tabs, tm, latent):
    b, s, d = x.shape
    nmod = scale.shape[0]
    mod_map = (lambda i, j: (i, 0, 0)) if nmod > 1 else (lambda i, j: (0, 0, 0))
    full = lambda a: pl.BlockSpec(a.shape, lambda i, j: (0,) * a.ndim)
    row = lambda n: pl.BlockSpec((1, tm, n), lambda i, j: (i, j, 0))
    tab = pl.BlockSpec((tm, LANES), lambda i, j: (j, 0))
    mod = pl.BlockSpec((1, 1, d), mod_map)
    kw = C_HEADS * LANES
    k_shape = jax.ShapeDtypeStruct((b, s, kw), BF16)
    vt_shape = jax.ShapeDtypeStruct((b, C_WIDTH, s), BF16)
    vt_spec = pl.BlockSpec((1, C_WIDTH, tm), lambda i, j: (i, 0, j))
    kv_w = [w["wkv"], w["kvn"], w["wk"], w["wvt"]]
    if latent:
        return pl.pallas_call(
            _in1_lat_body,
            grid=(b, s // tm),
            in_specs=[row(d), full(g), mod, mod, full(w["wqa"]), full(w["qn"]), full(w["wqb"])]
                     + [full(a) for a in kv_w] + [full(w["wg"]), tab, tab, tab],
            out_specs=[row(kw), row(kw), vt_spec, row(C_WIDTH)],
            out_shape=[k_shape, k_shape, vt_shape, jax.ShapeDtypeStruct((b, s, C_WIDTH), BF16)],
            compiler_params=_params(2),
            name="in1_lat",
        )(x, g, scale, shift, w["wqa"], w["qn"], w["wqb"], *kv_w, w["wg"], *tabs)
    return pl.pallas_call(
        _in1_ctx_body,
        grid=(b, s // tm),
        in_specs=[row(d), full(g), mod, mod] + [full(a) for a in kv_w],
        out_specs=[row(kw), vt_spec],
        out_shape=[k_shape, vt_shape],
        compiler_params=_params(2),
        name="in1_ctx",
    )(x, g, scale, shift, *kv_w)


ATTN1_KEY_CHUNK = 1024
ATTN1_HEADS = 8


def _attn1_body(q_ref, k_ref, kx_ref, vt_ref, vxt_ref, sg_ref, o_ref):
    nh = ATTN1_HEADS
    tiles = []
    for kr, vr in ((k_ref, vt_ref), (kx_ref, vxt_ref)):
        n_keys = kr.shape[1]
        tiles += [(kr, vr, lo, min(lo + ATTN1_KEY_CHUNK, n_keys)) for lo in range(0, n_keys, ATTN1_KEY_CHUNK)]
    state = [None] * nh

    def scores(n, t):
        kr, _, lo, hi = tiles[n]
        return _dot_nt(kr[0, lo:hi, t * LANES:(t + 1) * LANES],
                       q_ref[0, :, t * LANES:(t + 1) * LANES])

    s_next = [scores(0, t) for t in range(nh)]
    for n, (_, vr, lo, hi) in enumerate(tiles):
        for t in range(nh):
            s = s_next[t]
            if n + 1 < len(tiles):
                s_next[t] = scores(n + 1, t)
            v_t = vr[0, t * C_V:(t + 1) * C_V, lo:hi]
            m_tile = jnp.max(s, axis=0, keepdims=True)
            if state[t] is None:
                m = m_tile
                p = jnp.exp2(s - m)
                den = jnp.sum(p, axis=0, keepdims=True)
                acc = _dot(v_t, p.astype(BF16))
            else:
                m_old, den_old, acc_old = state[t]
                m = jnp.maximum(m_old, m_tile)
                alpha = jnp.exp2(m_old - m)
                p = jnp.exp2(s - m)
                den = den_old * alpha + jnp.sum(p, axis=0, keepdims=True)
                acc = acc_old * alpha + _dot(v_t, p.astype(BF16))
            state[t] = (m, den, acc)
    for t in range(0, nh, 2):
        o = jnp.concatenate([acc / den for _, den, acc in state[t:t + 2]], axis=0).T
        sl = slice(t * C_V, (t + 2) * C_V)
        o_ref[0, :, sl] = (o * sg_ref[0, :, sl].astype(F32)).astype(BF16)


def _attn1_call(q, k, kx, vt, vxt, sg, tq):
    b, s, _ = q.shape
    sx = kx.shape[1]
    nh = ATTN1_HEADS
    return pl.pallas_call(
        _attn1_body,
        grid=(b, C_HEADS // nh, s // tq),
        in_specs=[pl.BlockSpec((1, tq, nh * LANES), lambda i, h, j: (i, j, h)),
                  pl.BlockSpec((1, s, nh * LANES), lambda i, h, j: (i, 0, h)),
                  pl.BlockSpec((1, sx, nh * LANES), lambda i, h, j: (i, 0, h)),
                  pl.BlockSpec((1, nh * C_V, s), lambda i, h, j: (i, h, 0)),
                  pl.BlockSpec((1, nh * C_V, sx), lambda i, h, j: (i, h, 0)),
                  pl.BlockSpec((1, tq, nh * C_V), lambda i, h, j: (i, j, h))],
        out_specs=pl.BlockSpec((1, tq, nh * C_V), lambda i, h, j: (i, j, h)),
        out_shape=jax.ShapeDtypeStruct((b, s, C_WIDTH), BF16),
        compiler_params=_params(3),
        name="attn1",
    )(q, k, kx, vt, vxt, sg)


def _out1_body(og_ref, x_ref, gl_ref, wo_ref, fg_ref, o_ref):
    x = x_ref[0] + gl_ref[0] * _dot(og_ref[0], wo_ref[...])
    o_ref[0] = _rms(x, fg_ref[...])


def _out1_call(og, x, gate, wo, fg, tm):
    b, s, d = x.shape
    full = lambda a: pl.BlockSpec(a.shape, lambda i, j: (0,) * a.ndim)
    row = lambda n: pl.BlockSpec((1, tm, n), lambda i, j: (i, j, 0))
    return pl.pallas_call(
        _out1_body,
        grid=(b, s // tm),
        in_specs=[row(C_WIDTH), row(d), pl.BlockSpec((1, 1, d), lambda i, j: (i, 0, 0)), full(wo), full(fg)],
        out_specs=row(d),
        out_shape=jax.ShapeDtypeStruct((b, s, d), F32),
        compiler_params=_params(2),
        name="out1",
    )(og, x, gate, wo, fg)


def _rope_tables(seq, offset, width, half):
    t = np.arange(seq)
    pos = np.stack([t // GRID_W, t % GRID_W], axis=0).astype(np.float32)
    lane = np.arange(LANES)
    d = (lane - offset) % (2 * width) if offset else lane % (2 * width)
    active = np.ones(LANES, bool) if not offset else (lane >= offset) & (lane < offset + 2 * width)
    axis = d // width
    e = d % width
    f = e % half
    inv = (ROPE_BASE ** (-(np.arange(half, dtype=np.float32)) / half)).astype(np.float32)
    ang = pos[axis, :].T * inv[f][None, :]
    cos = np.where(active[None, :], np.cos(ang), 1.0)
    sin = np.where(active[None, :], np.sin(ang), 0.0)
    hi = (e >= half)[None, :]
    return (jnp.asarray(cos, F32), jnp.asarray(np.where(hi, sin, 0.0), F32),
            jnp.asarray(np.where(hi, 0.0, -sin), F32))


def _pair_perm():
    n = A_HEADS // A_KV_HEADS
    idx = []
    for j in range(n):
        idx += list(range(j * A_HEAD_DIM, (j + 1) * A_HEAD_DIM))
        idx += list(range((n + j) * A_HEAD_DIM, (n + j + 1) * A_HEAD_DIM))
    return np.asarray(idx)


def kernel(x, c, ctx, c_ctx, norm_g, w_ada, b_ada, w_in0, sink0, gm_ln_g, gm_ln_b, gm_ws, gm_bs, w_out0, w_in1,
           q_norm, w_qb, kv_norm, w_kvb, w_out1, final_g):
    b, s, d = x.shape
    sx = ctx.shape[1]

    cond = jnp.concatenate([c, c_ctx[None, :], jnp.zeros((16 - b - 1, d), F32)], axis=0)
    mod = _ada_call(cond, w_ada, b_ada)
    parts = lambda l, r0, r1: [mod[l, r0:r1, i * d:(i + 1) * d][:, None, :] for i in range(3)]

    perm = _pair_perm()
    mix_perm = np.concatenate([perm, A_WIDTH + np.arange(B_WIDTH)])
    o = A_WIDTH
    w0 = w_in0[0]
    wts0 = {
        "wq": w0[:, :o][:, perm].astype(BF16),
        "wk": w0[:, o:o + A_KV_WIDTH].astype(BF16),
        "wvt": w0[:, o + A_KV_WIDTH:o + 2 * A_KV_WIDTH].T.astype(BF16),
        "wu": w0[:, o + 2 * A_KV_WIDTH:o + 2 * A_KV_WIDTH + B_WIDTH].astype(BF16),
        "wvv": w0[:, o + 2 * A_KV_WIDTH + B_WIDTH:o + 2 * A_KV_WIDTH + 2 * B_WIDTH].astype(BF16),
        "wg": w0[:, o + 2 * A_KV_WIDTH + 2 * B_WIDTH:][:, mix_perm].astype(BF16),
        "lng": gm_ln_g[0][None, :], "lnb": gm_ln_b[0][None, :],
    }
    tabs0 = _rope_tables(s, 0, A_HEAD_DIM // 2, A_HEAD_DIM // 4)
    g0 = norm_g[0][None, :]
    sh_l, sc_l, gt_l = parts(0, 0, b)
    sh_c, sc_c, gt_c = parts(0, b, b + 1)
    q, k, v, gu, vn, sg = _in0_call(x, g0, sc_l, sh_l, wts0, tabs0, 512, True)
    qc, kc, vc, guc, vnc, sgc = _in0_call(ctx, g0, sc_c, sh_c, wts0, tabs0, sx, False)

    ws = gm_ws[0].astype(BF16)
    bs = jnp.repeat(gm_bs[0].T, B_GROUP_DIM, axis=1)
    wo0 = w_out0[0][mix_perm].astype(BF16)
    sink = sink0[0]
    x1 = _mix0_call(sink, q, k, v, kc, vc, gu, vn, sg, x, gt_l, ws, bs, wo0, True)
    xc1 = _mix0_call(sink, qc, None, None, kc, vc, guc, vnc, sgc, ctx, gt_c, ws, bs, wo0, False)

    w1 = w_in1[0]
    kvw = w_kvb[0].reshape(KV_RANK, C_HEADS, C_NOPE + C_V)
    wk = jnp.zeros((2 * LANES, C_HEADS, LANES), F32)
    wk = wk.at[:KV_RANK, :, :C_NOPE].set(kvw[:, :, :C_NOPE])
    eye = jnp.broadcast_to(jnp.eye(C_ROPE, dtype=F32)[:, None, :], (C_ROPE, C_HEADS, C_ROPE))
    wk = wk.at[KV_RANK + C_NOPE:KV_RANK + C_NOPE + C_ROPE, :, C_NOPE:C_NOPE + C_ROPE].set(eye)
    wkv = jnp.zeros((d, 2 * LANES), F32)
    wkv = wkv.at[:, :KV_RANK].set(w1[:, Q_RANK:Q_RANK + KV_RANK])
    wkv = wkv.at[:, KV_RANK + C_NOPE:KV_RANK + C_NOPE + C_ROPE].set(
        w1[:, Q_RANK + KV_RANK:Q_RANK + KV_RANK + C_ROPE])
    wqb = jnp.pad(w_qb[0].reshape(Q_RANK, C_HEADS, C_NOPE + C_ROPE),
                  ((0, 0), (0, 0), (0, LANES - C_NOPE - C_ROPE)))
    wts1 = {
        "wqa": w1[:, :Q_RANK].astype(BF16), "qn": q_norm[0][None, :],
        "wqb": wqb.reshape(Q_RANK, C_HEADS * LANES).astype(BF16),
        "wkv": wkv.astype(BF16), "kvn": kv_norm[0][None, :],
        "wk": wk.reshape(2 * LANES, C_HEADS * LANES).astype(BF16),
        "wvt": kvw[:, :, C_NOPE:].reshape(KV_RANK, C_WIDTH).T.astype(BF16),
        "wg": w1[:, Q_RANK + KV_RANK + C_ROPE:].astype(BF16),
    }
    tabs1 = _rope_tables(s, C_NOPE, C_ROPE // 2, C_ROPE // 4)
    g1 = norm_g[1][None, :]
    sh_l, sc_l, gt_l = parts(1, 0, b)
    sh_c, sc_c, _ = parts(1, b, b + 1)
    q1, k1, vt1, sg1 = _in1_call(x1, g1, sc_l, sh_l, wts1, tabs1, 512, True)
    kx1, vxt1 = _in1_call(xc1, g1, sc_c, sh_c, wts1, None, sx, False)
    og = _attn1_call(q1, k1, kx1, vt1, vxt1, sg1, 256)
    return _out1_call(og, x1, gt_l, w_out1[0].astype(BF16), final_g[None, :], 512)
```

```python
import functools

import numpy as np
import jax
import jax.numpy as jnp
from jax import lax
from jax.experimental import pallas as pl
from jax.experimental.pallas import tpu as pltpu

D_MODEL = 1024
GRID_W = 64
ROPE_BASE = 10000.0
NORM_EPS = 1e-6
NEG_INF = -1e30

A_HEADS = 8
A_KV_HEADS = 2
A_HEAD_DIM = 64
A_WIDTH = A_HEADS * A_HEAD_DIM
A_KV_WIDTH = A_KV_HEADS * A_HEAD_DIM
A_BLOCK = 128
B_GROUPS = 8
B_WIDTH = 512
B_GROUP_DIM = 64
EVEN_MIX = A_WIDTH + B_WIDTH

C_HEADS = 16
C_NOPE = 64
C_ROPE = 32
C_V = 64
Q_RANK = 256
KV_RANK = 128
C_WIDTH = C_HEADS * C_V

LANES = 128
VMEM_LIMIT = 48 * 1024 * 1024

LOG2_E = 1.4426950408889634

BF16 = jnp.bfloat16
F32 = jnp.float32


def _params(n_axes):
    return pltpu.CompilerParams(dimension_semantics=("parallel",) * n_axes,
                                vmem_limit_bytes=VMEM_LIMIT)


def _dot(a, b):
    return jnp.dot(a, b, preferred_element_type=F32)


def _dot_nt(a, b):
    return lax.dot_general(a, b, (((1,), (1,)), ((), ())), preferred_element_type=F32)


def _rope(x, cos, sin_hi, sin_lo, half):
    return (x * cos + pltpu.roll(x, half, 1) * sin_hi
            + pltpu.roll(x, LANES - half, 1) * sin_lo)


def _gelu(x):
    return 0.5 * x * (1.0 + lax.erf(x * (2.0 ** -0.5)))


def _modulated_rms(x, g, scale, shift):
    r = lax.rsqrt(jnp.mean(x * x, axis=-1, keepdims=True) + NORM_EPS)
    return (x * r) * (g * (1.0 + scale)) + shift


def _ada_body(c_ref, w_ref, b_ref, o_ref):
    c = c_ref[...]
    cs = c * jax.nn.sigmoid(c)
    o_ref[0] = _dot(cs, w_ref[0]) + b_ref[0]


def _ada_call(cond, w_ada, b_ada):
    depth, d, n3 = w_ada.shape
    rows = cond.shape[0]
    tn = 768
    return pl.pallas_call(
        _ada_body,
        grid=(depth, n3 // tn),
        in_specs=[pl.BlockSpec((rows, d), lambda l, j: (0, 0)),
                  pl.BlockSpec((1, d, tn), lambda l, j: (l, 0, j)),
                  pl.BlockSpec((1, 1, tn), lambda l, j: (l, 0, j))],
        out_specs=pl.BlockSpec((1, rows, tn), lambda l, j: (l, 0, j)),
        out_shape=jax.ShapeDtypeStruct((depth, rows, n3), F32),
        compiler_params=_params(2),
        name="ada",
    )(cond, w_ada, b_ada.reshape(depth, 1, n3))


def _in0_body(use_rope, x_ref, g_ref, sc_ref, sh_ref, wq_ref, wk_ref, wvt_ref, wu_ref, wvv_ref, wg_ref,
              lng_ref, lnb_ref, cos_ref, shi_ref, slo_ref,
              q_ref, k_ref, vt_ref, gu_ref, vn_ref, sg_ref):
    h = _modulated_rms(x_ref[0], g_ref[...], sc_ref[0], sh_ref[0]).astype(BF16)
    if use_rope:
        cos, shi, slo = cos_ref[...], shi_ref[...], slo_ref[...]
        rope = lambda t: _rope(t, cos, shi, slo, 16)
    else:
        rope = lambda t: t
    scale = A_HEAD_DIM ** -0.5 * LOG2_E
    q = _dot(h, wq_ref[...])
    for j in range(A_WIDTH // LANES):
        sl = slice(j * LANES, (j + 1) * LANES)
        q_ref[0, :, sl] = (rope(q[:, sl]) * scale).astype(BF16)
    k_ref[0] = rope(_dot(h, wk_ref[...])).astype(BF16)
    vt_ref[0] = _dot_nt(wvt_ref[...], h).astype(BF16)
    gu_ref[0] = _gelu(_dot(h, wu_ref[...])).astype(BF16)
    gv = _gelu(_dot(h, wvv_ref[...]))
    mu = jnp.mean(gv, axis=-1, keepdims=True)
    gc = gv - mu
    vn = gc * lax.rsqrt(jnp.mean(gc * gc, axis=-1, keepdims=True) + NORM_EPS)
    vn_ref[0] = (vn * lng_ref[...] + lnb_ref[...]).astype(BF16)
    gate = _dot(h, wg_ref[...])
    sg_ref[0] = (gate * jax.nn.sigmoid(gate)).astype(BF16)


def _in0_call(x, g, scale, shift, w, tabs, tm, use_rope):
    b, s, d = x.shape
    nmod = scale.shape[0]
    mod_map = (lambda i, j: (i, 0, 0)) if nmod > 1 else (lambda i, j: (0, 0, 0))
    full = lambda a: pl.BlockSpec(a.shape, lambda i, j: (0,) * a.ndim)
    row = lambda n: pl.BlockSpec((1, tm, n), lambda i, j: (i, j, 0))
    tab = pl.BlockSpec((tm, LANES), lambda i, j: (j, 0))
    widths = [A_WIDTH, A_KV_WIDTH, None, B_WIDTH, B_WIDTH, EVEN_MIX]
    vt_spec = pl.BlockSpec((1, A_KV_WIDTH, tm), lambda i, j: (i, 0, j))
    vt_shape = jax.ShapeDtypeStruct((b, A_KV_WIDTH, s), BF16)
    return pl.pallas_call(
        functools.partial(_in0_body, use_rope),
        grid=(b, s // tm),
        in_specs=[row(d), full(g), pl.BlockSpec((1, 1, d), mod_map), pl.BlockSpec((1, 1, d), mod_map),
                  full(w["wq"]), full(w["wk"]), full(w["wvt"]), full(w["wu"]), full(w["wvv"]), full(w["wg"]),
                  full(w["lng"]), full(w["lnb"]), tab, tab, tab],
        out_specs=[vt_spec if n is None else row(n) for n in widths],
        out_shape=[vt_shape if n is None else jax.ShapeDtypeStruct((b, s, n), BF16) for n in widths],
        compiler_params=_params(2),
        name="in0_lat" if use_rope else "in0_ctx",
    )(x, g, scale, shift, w["wq"], w["wk"], w["wvt"], w["wu"], w["wvv"], w["wg"], w["lng"], w["lnb"], *tabs)


def _mix0_body(has_local, sink_ref, q_ref, *refs):
    if has_local:
        kp_ref, kc_ref, kn_ref, vp_ref, vc_ref, vn_ref, *refs = refs
    kx_ref, vx_ref, gu_ref, vnrm_ref, sg_ref, x_ref, gl_ref, ws_ref, bs_ref, wo_ref, o_ref = refs
    rows = q_ref.shape[1]
    qrows = A_BLOCK if has_local else rows
    nblk = rows // qrows
    npair = A_WIDTH // LANES
    ncol = npair * qrows
    nx = kx_ref.shape[1]
    lane_lo = lax.broadcasted_iota(jnp.int32, (qrows, LANES), 1) < A_HEAD_DIM
    chunk_lane_lo = lax.broadcasted_iota(jnp.int32, (A_BLOCK, LANES), 1) < A_HEAD_DIM
    if has_local:
        i = pl.program_id(1)
        nstep = pl.num_programs(1)
        key = lax.broadcasted_iota(jnp.int32, (A_BLOCK, ncol), 0)
        qry = lax.broadcasted_iota(jnp.int32, (A_BLOCK, ncol), 1) & (A_BLOCK - 1)
        band_prev = key >= qry
        band_next = key <= qry
        blk = lambda a: slice(a * A_BLOCK, (a + 1) * A_BLOCK)
        k_win = [kp_ref[0]] + [kc_ref[0, blk(a), :] for a in range(nblk)] + [kn_ref[0]]
        vt_win = [vp_ref[0]] + [vc_ref[0, :, blk(a)] for a in range(nblk)] + [vn_ref[0]]

    streams = []
    for a in range(nblk):
        if has_local:
            k_all = jnp.concatenate([kx_ref[0]] + k_win[a:a + 3], axis=0)
            vt_all = jnp.concatenate([vx_ref[0]] + vt_win[a:a + 3], axis=1)
        else:
            k_all, vt_all = kx_ref[0], vx_ref[0]
        q_pairs = [q_ref[0, a * qrows:(a + 1) * qrows, jp * LANES:(jp + 1) * LANES] for jp in range(npair)]
        for grp in range(A_KV_HEADS):
            keep = lane_lo if grp == 0 else ~lane_lo
            qm = jnp.concatenate([jnp.where(keep, qp, jnp.zeros_like(qp)) for qp in q_pairs], axis=0)
            streams.append((a, grp, _dot_nt(k_all, qm), vt_all))

    gmlp = []
    for jp in range(B_WIDTH // LANES):
        sl = slice(jp * LANES, (jp + 1) * LANES)
        chunks = []
        for ch in range(rows // A_BLOCK):
            rs = slice(ch * A_BLOCK, (ch + 1) * A_BLOCK)
            vp = vnrm_ref[0, rs, sl]
            s0 = _dot(ws_ref[2 * jp], vp)
            s1 = _dot(ws_ref[2 * jp + 1], vp)
            chunks.append(jnp.where(chunk_lane_lo, s0, s1) + bs_ref[:, sl])
        sg = chunks[0] if len(chunks) == 1 else jnp.concatenate(chunks, axis=0)
        gmlp.append(gu_ref[0, :, sl].astype(F32) * sg)

    out_t = {}
    for a, grp, s_t, vt_all in streams:
        if has_local:
            has_prev = (i > 0) if a == 0 else True
            has_next = (i < nstep - 1) if a == nblk - 1 else True
            parts = [s_t[:nx],
                     jnp.where(band_prev & has_prev, s_t[nx:nx + A_BLOCK], NEG_INF),
                     s_t[nx + A_BLOCK:nx + 2 * A_BLOCK],
                     jnp.where(band_next & has_next, s_t[nx + 2 * A_BLOCK:], NEG_INF)]
        else:
            parts = [s_t]
        sink = sink_ref[grp:grp + 1, :] * LOG2_E
        m = sink
        for part in parts:
            m = jnp.maximum(m, jnp.max(part, axis=0, keepdims=True))
        den = jnp.exp2(sink - m)
        probs = []
        for part in parts:
            p = jnp.exp2(part - m)
            den = den + jnp.sum(p, axis=0, keepdims=True)
            probs.append(p.astype(BF16))
        p_t = probs[0] if len(probs) == 1 else jnp.concatenate(probs, axis=0)
        out_t[a, grp] = _dot(vt_all, p_t) / den

    row_lo = lax.broadcasted_iota(jnp.int32, (LANES, qrows), 0) < A_HEAD_DIM
    pieces = []
    for jp in range(npair):
        cs = slice(jp * qrows, (jp + 1) * qrows)
        blocks = [jnp.where(row_lo, out_t[a, 0][:, cs], out_t[a, 1][:, cs]).T for a in range(nblk)]
        pieces.append(blocks[0] if nblk == 1 else jnp.concatenate(blocks, axis=0))
    pieces += gmlp

    mixed = jnp.concatenate(pieces, axis=-1) * sg_ref[0].astype(F32)
    y = _dot(mixed.astype(BF16), wo_ref[...])
    o_ref[0] = x_ref[0] + gl_ref[0] * y


MIX0_ROWS = 512


def _mix0_call(sink, q, k, vt, kx, vxt, gu, vnrm, sg, x, gate, ws, bs, wo, has_local):
    b, s, d = x.shape
    rows = MIX0_ROWS if has_local else s
    qrows = A_BLOCK if has_local else s
    nstep = s // rows
    nmod = gate.shape[0]
    mod_map = (lambda i, j: (i, 0, 0)) if nmod > 1 else (lambda i, j: (0, 0, 0))
    full = lambda a: pl.BlockSpec(a.shape, lambda i, j: (0,) * a.ndim)
    row = lambda n: pl.BlockSpec((1, rows, n), lambda i, j: (i, j, 0))
    ctx = lambda a: pl.BlockSpec((1,) + a.shape[1:], lambda i, j: (i, 0, 0))
    sink_rows = jnp.repeat(sink.reshape(A_KV_HEADS, A_HEADS // A_KV_HEADS), qrows, axis=1)
    in_specs = [full(sink_rows), row(A_WIDTH)]
    args = [sink_rows, q]
    if has_local:
        per = rows // A_BLOCK
        last = s // A_BLOCK - 1
        prev = lambda j: jnp.maximum(j * per - 1, 0)
        nxt = lambda j: jnp.minimum((j + 1) * per, last)
        kblk = lambda f: pl.BlockSpec((1, A_BLOCK, A_KV_WIDTH), lambda i, j: (i, f(j), 0))
        vblk = lambda f: pl.BlockSpec((1, A_KV_WIDTH, A_BLOCK), lambda i, j: (i, 0, f(j)))
        in_specs += [kblk(prev), row(A_KV_WIDTH), kblk(nxt),
                     vblk(prev), pl.BlockSpec((1, A_KV_WIDTH, rows), lambda i, j: (i, 0, j)), vblk(nxt)]
        args += [k, k, k, vt, vt, vt]
    in_specs += [ctx(kx), ctx(vxt), row(B_WIDTH), row(B_WIDTH), row(EVEN_MIX), row(d),
                 pl.BlockSpec((1, 1, d), mod_map), full(ws), full(bs), full(wo)]
    args += [kx, vxt, gu, vnrm, sg, x, gate, ws, bs, wo]
    return pl.pallas_call(
        functools.partial(_mix0_body, has_local),
        grid=(b, nstep),
        in_specs=in_specs,
        out_specs=row(d),
        out_shape=jax.ShapeDtypeStruct((b, s, d), F32),
        compiler_params=_params(2),
        name="mix0_lat" if has_local else "mix0_ctx",
    )(*args)


def _rms(x, g):
    return x * lax.rsqrt(jnp.mean(x * x, axis=-1, keepdims=True) + NORM_EPS) * g


def _in1_kv(h, use_rope, wkv_ref, kvn_ref, wk_ref, wvt_ref, tabs, k_ref, vt_ref):
    kva = _dot(h, wkv_ref[...])
    kvn = _rms(kva[:, :KV_RANK], kvn_ref[...]).astype(BF16)
    kpe = kva[:, KV_RANK:]
    if use_rope:
        kpe = _rope(kpe, *tabs, 8)
    kin = jnp.concatenate([kvn, kpe.astype(BF16)], axis=-1)
    k_ref[0] = _dot(kin, wk_ref[...]).astype(BF16)
    vt_ref[0] = _dot_nt(wvt_ref[...], kvn).astype(BF16)


def _in1_lat_body(x_ref, g_ref, sc_ref, sh_ref, wqa_ref, qn_ref, wqb_ref, wkv_ref, kvn_ref, wk_ref, wvt_ref,
                  wg_ref, cos_ref, shi_ref, slo_ref, q_ref, k_ref, vt_ref, sg_ref):
    h = _modulated_rms(x_ref[0], g_ref[...], sc_ref[0], sh_ref[0]).astype(BF16)
    tabs = (cos_ref[...], shi_ref[...], slo_ref[...])
    qn = _rms(_dot(h, wqa_ref[...]), qn_ref[...]).astype(BF16)
    scale = (C_NOPE + C_ROPE) ** -0.5 * LOG2_E
    for hp in range(C_HEADS // 2):
        q2 = _dot(qn, wqb_ref[:, hp * 2 * LANES:(hp + 1) * 2 * LANES])
        for t in range(2):
            sl = slice(t * LANES, (t + 1) * LANES)
            q_ref[0, :, (2 * hp + t) * LANES:(2 * hp + t + 1) * LANES] = (
                _rope(q2[:, sl], *tabs, 8) * scale).astype(BF16)
    _in1_kv(h, True, wkv_ref, kvn_ref, wk_ref, wvt_ref, tabs, k_ref, vt_ref)
    gate = _dot(h, wg_ref[...])
    sg_ref[0] = (gate * jax.nn.sigmoid(gate)).astype(BF16)


def _in1_ctx_body(x_ref, g_ref, sc_ref, sh_ref, wkv_ref, kvn_ref, wk_ref, wvt_ref, k_ref, vt_ref):
    h = _modulated_rms(x_ref[0], g_ref[...], sc_ref[0], sh_ref[0]).astype(BF16)
    _in1_kv(h, False, wkv_ref, kvn_ref, wk_ref, wvt_ref, None, k_ref, vt_ref)


def _in1_call(x, g, scale, shift, w, tabs, tm, latent):
    b, s, d = x.shape
    nmod = scale.shape[0]
    mod_map = (lambda i, j: (i, 0, 0)) if nmod > 1 else (lambda i, j: (0, 0, 0))
    full = lambda a: pl.BlockSpec(a.shape, lambda i, j: (0,) * a.ndim)
    row = lambda n: pl.BlockSpec((1, tm, n), lambda i, j: (i, j, 0))
    tab = pl.BlockSpec((tm, LANES), lambda i, j: (j, 0))
    mod = pl.BlockSpec((1, 1, d), mod_map)
    kw = C_HEADS * LANES
    k_shape = jax.ShapeDtypeStruct((b, s, kw), BF16)
    vt_shape = jax.ShapeDtypeStruct((b, C_WIDTH, s), BF16)
    vt_spec = pl.BlockSpec((1, C_WIDTH, tm), lambda i, j: (i, 0, j))
    kv_w = [w["wkv"], w["kvn"], w["wk"], w["wvt"]]
    if latent:
        return pl.pallas_call(
            _in1_lat_body,
            grid=(b, s // tm),
            in_specs=[row(d), full(g), mod, mod, full(w["wqa"]), full(w["qn"]), full(w["wqb"])]
                     + [full(a) for a in kv_w] + [full(w["wg"]), tab, tab, tab],
            out_specs=[row(kw), row(kw), vt_spec, row(C_WIDTH)],
            out_shape=[k_shape, k_shape, vt_shape, jax.ShapeDtypeStruct((b, s, C_WIDTH), BF16)],
            compiler_params=_params(2),
            name="in1_lat",
        )(x, g, scale, shift, w["wqa"], w["qn"], w["wqb"], *kv_w, w["wg"], *tabs)
    return pl.pallas_call(
        _in1_ctx_body,
        grid=(b, s // tm),
        in_specs=[row(d), full(g), mod, mod] + [full(a) for a in kv_w],
        out_specs=[row(kw), vt_spec],
        out_shape=[k_shape, vt_shape],
        compiler_params=_params(2),
        name="in1_ctx",
    )(x, g, scale, shift, *kv_w)


ATTN1_KEY_CHUNK = 1024
ATTN1_HEADS = 8


def _attn1_body(q_ref, k_ref, kx_ref, vt_ref, vxt_ref, sg_ref, o_ref):
    nh = ATTN1_HEADS
    tiles = []
    for kr, vr in ((k_ref, vt_ref), (kx_ref, vxt_ref)):
        n_keys = kr.shape[1]
        tiles += [(kr, vr, lo, min(lo + ATTN1_KEY_CHUNK, n_keys)) for lo in range(0, n_keys, ATTN1_KEY_CHUNK)]
    state = [None] * nh

    def scores(n, t):
        kr, _, lo, hi = tiles[n]
        return _dot_nt(kr[0, lo:hi, t * LANES:(t + 1) * LANES],
                       q_ref[0, :, t * LANES:(t + 1) * LANES])

    s_next = [scores(0, t) for t in range(nh)]
    for n, (_, vr, lo, hi) in enumerate(tiles):
        for t in range(nh):
            s = s_next[t]
            if n + 1 < len(tiles):
                s_next[t] = scores(n + 1, t)
            v_t = vr[0, t * C_V:(t + 1) * C_V, lo:hi]
            m_tile = jnp.max(s, axis=0, keepdims=True)
            if state[t] is None:
                m = m_tile
                p = jnp.exp2(s - m)
                den = jnp.sum(p, axis=0, keepdims=True)
                acc = _dot(v_t, p.astype(BF16))
            else:
                m_old, den_old, acc_old = state[t]
                m = jnp.maximum(m_old, m_tile)
                alpha = jnp.exp2(m_old - m)
                p = jnp.exp2(s - m)
                den = den_old * alpha + jnp.sum(p, axis=0, keepdims=True)
                acc = acc_old * alpha + _dot(v_t, p.astype(BF16))
            state[t] = (m, den, acc)
    for t in range(0, nh, 2):
        o = jnp.concatenate([acc / den for _, den, acc in state[t:t + 2]], axis=0).T
        sl = slice(t * C_V, (t + 2) * C_V)
        o_ref[0, :, sl] = (o * sg_ref[0, :, sl].astype(F32)).astype(BF16)


def _attn1_call(q, k, kx, vt, vxt, sg, tq):
    b, s, _ = q.shape
    sx = kx.shape[1]
    nh = ATTN1_HEADS
    return pl.pallas_call(
        _attn1_body,
        grid=(b, C_HEADS // nh, s // tq),
        in_specs=[pl.BlockSpec((1, tq, nh * LANES), lambda i, h, j: (i, j, h)),
                  pl.BlockSpec((1, s, nh * LANES), lambda i, h, j: (i, 0, h)),
                  pl.BlockSpec((1, sx, nh * LANES), lambda i, h, j: (i, 0, h)),
                  pl.BlockSpec((1, nh * C_V, s), lambda i, h, j: (i, h, 0)),
                  pl.BlockSpec((1, nh * C_V, sx), lambda i, h, j: (i, h, 0)),
                  pl.BlockSpec((1, tq, nh * C_V), lambda i, h, j: (i, j, h))],
        out_specs=pl.BlockSpec((1, tq, nh * C_V), lambda i, h, j: (i, j, h)),
        out_shape=jax.ShapeDtypeStruct((b, s, C_WIDTH), BF16),
        compiler_params=_params(3),
        name="attn1",
    )(q, k, kx, vt, vxt, sg)


def _out1_body(og_ref, x_ref, gl_ref, wo_ref, fg_ref, o_ref):
    x = x_ref[0] + gl_ref[0] * _dot(og_ref[0], wo_ref[...])
    o_ref[0] = _rms(x, fg_ref[...])


def _out1_call(og, x, gate, wo, fg, tm):
    b, s, d = x.shape
    full = lambda a: pl.BlockSpec(a.shape, lambda i, j: (0,) * a.ndim)
    row = lambda n: pl.BlockSpec((1, tm, n), lambda i, j: (i, j, 0))
    return pl.pallas_call(
        _out1_body,
        grid=(b, s // tm),
        in_specs=[row(C_WIDTH), row(d), pl.BlockSpec((1, 1, d), lambda i, j: (i, 0, 0)), full(wo), full(fg)],
        out_specs=row(d),
        out_shape=jax.ShapeDtypeStruct((b, s, d), F32),
        compiler_params=_params(2),
        name="out1",
    )(og, x, gate, wo, fg)


def _rope_tables(seq, offset, width, half):
    t = np.arange(seq)
    pos = np.stack([t // GRID_W, t % GRID_W], axis=0).astype(np.float32)
    lane = np.arange(LANES)
    d = (lane - offset) % (2 * width) if offset else lane % (2 * width)
    active = np.ones(LANES, bool) if not offset else (lane >= offset) & (lane < offset + 2 * width)
    axis = d // width
    e = d % width
    f = e % half
    inv = (ROPE_BASE ** (-(np.arange(half, dtype=np.float32)) / half)).astype(np.float32)
    ang = pos[axis, :].T * inv[f][None, :]
    cos = np.where(active[None, :], np.cos(ang), 1.0)
    sin = np.where(active[None, :], np.sin(ang), 0.0)
    hi = (e >= half)[None, :]
    return (jnp.asarray(cos, F32), jnp.asarray(np.where(hi, sin, 0.0), F32),
            jnp.asarray(np.where(hi, 0.0, -sin), F32))


def _pair_heads(w, axis):
    shape = w.shape
    w = w.reshape(shape[:axis] + (A_KV_HEADS, A_HEADS // A_KV_HEADS, A_HEAD_DIM) + shape[axis + 1:])
    return jnp.swapaxes(w, axis, axis + 1).reshape(shape)


def kernel(x, c, ctx, c_ctx, norm_g, w_ada, b_ada, w_in0, sink0, gm_ln_g, gm_ln_b, gm_ws, gm_bs, w_out0, w_in1,
           q_norm, w_qb, kv_norm, w_kvb, w_out1, final_g):
    b, s, d = x.shape
    sx = ctx.shape[1]

    cond = jnp.concatenate([c, c_ctx[None, :], jnp.zeros((16 - b - 1, d), F32)], axis=0)
    mod = _ada_call(cond, w_ada, b_ada)
    parts = lambda l, r0, r1: [mod[l, r0:r1, i * d:(i + 1) * d][:, None, :] for i in range(3)]

    o = A_WIDTH
    w0 = w_in0[0].astype(BF16)
    wg0 = w0[:, o + 2 * A_KV_WIDTH + 2 * B_WIDTH:]
    wts0 = {
        "wq": _pair_heads(w0[:, :o], 1),
        "wk": w0[:, o:o + A_KV_WIDTH],
        "wvt": w0[:, o + A_KV_WIDTH:o + 2 * A_KV_WIDTH].T,
        "wu": w0[:, o + 2 * A_KV_WIDTH:o + 2 * A_KV_WIDTH + B_WIDTH],
        "wvv": w0[:, o + 2 * A_KV_WIDTH + B_WIDTH:o + 2 * A_KV_WIDTH + 2 * B_WIDTH],
        "wg": jnp.concatenate([_pair_heads(wg0[:, :A_WIDTH], 1), wg0[:, A_WIDTH:]], axis=1),
        "lng": gm_ln_g[0][None, :], "lnb": gm_ln_b[0][None, :],
    }
    tabs0 = _rope_tables(s, 0, A_HEAD_DIM // 2, A_HEAD_DIM // 4)
    g0 = norm_g[0][None, :]
    sh_l, sc_l, gt_l = parts(0, 0, b)
    sh_c, sc_c, gt_c = parts(0, b, b + 1)
    q, k, v, gu, vn, sg = _in0_call(x, g0, sc_l, sh_l, wts0, tabs0, 512, True)
    qc, kc, vc, guc, vnc, sgc = _in0_call(ctx, g0, sc_c, sh_c, wts0, tabs0, sx, False)

    ws = gm_ws[0].astype(BF16)
    bs = jnp.repeat(gm_bs[0].T, B_GROUP_DIM, axis=1)
    wo0 = w_out0[0].astype(BF16)
    wo0 = jnp.concatenate([_pair_heads(wo0[:A_WIDTH], 0), wo0[A_WIDTH:]], axis=0)
    sink = sink0[0]
    x1 = _mix0_call(sink, q, k, v, kc, vc, gu, vn, sg, x, gt_l, ws, bs, wo0, True)
    xc1 = _mix0_call(sink, qc, None, None, kc, vc, guc, vnc, sgc, ctx, gt_c, ws, bs, wo0, False)

    w1 = w_in1[0].astype(BF16)
    kvw = w_kvb[0].astype(BF16).reshape(KV_RANK, C_HEADS, C_NOPE + C_V)
    place = np.zeros((LANES, C_HEADS, LANES), np.float32)
    for e in range(C_ROPE):
        place[C_NOPE + e, :, C_NOPE + e] = 1.0
    wk = jnp.concatenate([jnp.pad(kvw[:, :, :C_NOPE], ((0, 0), (0, 0), (0, LANES - C_NOPE))),
                          jnp.asarray(place, BF16)], axis=0)
    wkv = jnp.concatenate([w1[:, Q_RANK:Q_RANK + KV_RANK], jnp.zeros((d, C_NOPE), BF16),
                           w1[:, Q_RANK + KV_RANK:Q_RANK + KV_RANK + C_ROPE],
                           jnp.zeros((d, LANES - C_NOPE - C_ROPE), BF16)], axis=1)
    wqb = jnp.pad(w_qb[0].astype(BF16).reshape(Q_RANK, C_HEADS, C_NOPE + C_ROPE),
                  ((0, 0), (0, 0), (0, LANES - C_NOPE - C_ROPE)))
    wts1 = {
        "wqa": w1[:, :Q_RANK], "qn": q_norm[0][None, :],
        "wqb": wqb.reshape(Q_RANK, C_HEADS * LANES),
        "wkv": wkv, "kvn": kv_norm[0][None, :],
        "wk": wk.reshape(2 * LANES, C_HEADS * LANES),
        "wvt": kvw[:, :, C_NOPE:].reshape(KV_RANK, C_WIDTH).T,
        "wg": w1[:, Q_RANK + KV_RANK + C_ROPE:],
    }
    tabs1 = _rope_tables(s, C_NOPE, C_ROPE // 2, C_ROPE // 4)
    g1 = norm_g[1][None, :]
    sh_l, sc_l, gt_l = parts(1, 0, b)
    sh_c, sc_c, _ = parts(1, b, b + 1)
    q1, k1, vt1, sg1 = _in1_call(x1, g1, sc_l, sh_l, wts1, tabs1, 512, True)
    kx1, vxt1 = _in1_call(xc1, g1, sc_c, sh_c, wts1, None, sx, False)
    og = _attn1_call(q1, k1, kx1, vt1, vxt1, sg1, 256)
    return _out1_call(og, x1, gt_l, w_out1[0].astype(BF16), final_g[None, :], 512)
```

```python
import functools

import numpy as np
import jax
import jax.numpy as jnp
from jax import lax
from jax.experimental import pallas as pl
from jax.experimental.pallas import tpu as pltpu

D_MODEL = 1024
GRID_W = 64
ROPE_BASE = 10000.0
NORM_EPS = 1e-6
NEG_INF = -1e30

A_HEADS = 8
A_KV_HEADS = 2
A_HEAD_DIM = 64
A_WIDTH = A_HEADS * A_HEAD_DIM
A_KV_WIDTH = A_KV_HEADS * A_HEAD_DIM
A_BLOCK = 128
B_GROUPS = 8
B_WIDTH = 512
B_GROUP_DIM = 64
EVEN_MIX = A_WIDTH + B_WIDTH

C_HEADS = 16
C_NOPE = 64
C_ROPE = 32
C_V = 64
Q_RANK = 256
KV_RANK = 128
C_WIDTH = C_HEADS * C_V

LANES = 128
VMEM_LIMIT = 48 * 1024 * 1024

LOG2_E = 1.4426950408889634

BF16 = jnp.bfloat16
F32 = jnp.float32


def _params(n_axes):
    return pltpu.CompilerParams(dimension_semantics=("parallel",) * n_axes,
                                vmem_limit_bytes=VMEM_LIMIT)


def _dot(a, b):
    return jnp.dot(a, b, preferred_element_type=F32)


def _dot_nt(a, b):
    return lax.dot_general(a, b, (((1,), (1,)), ((), ())), preferred_element_type=F32)


def _rope(x, cos, sin_hi, sin_lo, half):
    return (x * cos + pltpu.roll(x, half, 1) * sin_hi
            + pltpu.roll(x, LANES - half, 1) * sin_lo)


def _gelu(x):
    return 0.5 * x * (1.0 + lax.erf(x * (2.0 ** -0.5)))


def _modulated_rms(x, g, scale, shift):
    r = lax.rsqrt(jnp.mean(x * x, axis=-1, keepdims=True) + NORM_EPS)
    return (x * r) * (g * (1.0 + scale)) + shift


def _ada_body(c_ref, w_ref, b_ref, o_ref):
    c = c_ref[...]
    cs = c * jax.nn.sigmoid(c)
    o_ref[0] = _dot(cs, w_ref[0]) + b_ref[0]


def _ada_call(cond, w_ada, b_ada):
    depth, d, n3 = w_ada.shape
    rows = cond.shape[0]
    tn = 768
    return pl.pallas_call(
        _ada_body,
        grid=(depth, n3 // tn),
        in_specs=[pl.BlockSpec((rows, d), lambda l, j: (0, 0)),
                  pl.BlockSpec((1, d, tn), lambda l, j: (l, 0, j)),
                  pl.BlockSpec((1, 1, tn), lambda l, j: (l, 0, j))],
        out_specs=pl.BlockSpec((1, rows, tn), lambda l, j: (l, 0, j)),
        out_shape=jax.ShapeDtypeStruct((depth, rows, n3), F32),
        compiler_params=_params(2),
        name="ada",
    )(cond, w_ada, b_ada.reshape(depth, 1, n3))


def _in0_body(use_rope, x_ref, g_ref, sc_ref, sh_ref, wq_ref, wk_ref, wvt_ref, wu_ref, wvv_ref, wg_ref,
              lng_ref, lnb_ref, cos_ref, shi_ref, slo_ref,
              q_ref, k_ref, vt_ref, gu_ref, vn_ref, sg_ref):
    h = _modulated_rms(x_ref[0], g_ref[...], sc_ref[0], sh_ref[0]).astype(BF16)
    if use_rope:
        cos, shi, slo = cos_ref[...], shi_ref[...], slo_ref[...]
        rope = lambda t: _rope(t, cos, shi, slo, 16)
    else:
        rope = lambda t: t
    scale = A_HEAD_DIM ** -0.5 * LOG2_E
    q = _dot(h, wq_ref[...])
    for j in range(A_WIDTH // LANES):
        sl = slice(j * LANES, (j + 1) * LANES)
        q_ref[0, :, sl] = (rope(q[:, sl]) * scale).astype(BF16)
    k_ref[0] = rope(_dot(h, wk_ref[...])).astype(BF16)
    vt_ref[0] = _dot_nt(wvt_ref[...], h).astype(BF16)
    gu_ref[0] = _gelu(_dot(h, wu_ref[...])).astype(BF16)
    gv = _gelu(_dot(h, wvv_ref[...]))
    mu = jnp.mean(gv, axis=-1, keepdims=True)
    gc = gv - mu
    vn = gc * lax.rsqrt(jnp.mean(gc * gc, axis=-1, keepdims=True) + NORM_EPS)
    vn_ref[0] = (vn * lng_ref[...] + lnb_ref[...]).astype(BF16)
    gate = _dot(h, wg_ref[...])
    sg_ref[0] = (gate * jax.nn.sigmoid(gate)).astype(BF16)


def _in0_call(x, g, scale, shift, w, tabs, tm, use_rope):
    b, s, d = x.shape
    nmod = scale.shape[0]
    mod_map = (lambda i, j: (i, 0, 0)) if nmod > 1 else (lambda i, j: (0, 0, 0))
    full = lambda a: pl.BlockSpec(a.shape, lambda i, j: (0,) * a.ndim)
    row = lambda n: pl.BlockSpec((1, tm, n), lambda i, j: (i, j, 0))
    tab = pl.BlockSpec((tm, LANES), lambda i, j: (j, 0))
    widths = [A_WIDTH, A_KV_WIDTH, None, B_WIDTH, B_WIDTH, EVEN_MIX]
    vt_spec = pl.BlockSpec((1, A_KV_WIDTH, tm), lambda i, j: (i, 0, j))
    vt_shape = jax.ShapeDtypeStruct((b, A_KV_WIDTH, s), BF16)
    return pl.pallas_call(
        functools.partial(_in0_body, use_rope),
        grid=(b, s // tm),
        in_specs=[row(d), full(g), pl.BlockSpec((1, 1, d), mod_map), pl.BlockSpec((1, 1, d), mod_map),
                  full(w["wq"]), full(w["wk"]), full(w["wvt"]), full(w["wu"]), full(w["wvv"]), full(w["wg"]),
                  full(w["lng"]), full(w["lnb"]), tab, tab, tab],
        out_specs=[vt_spec if n is None else row(n) for n in widths],
        out_shape=[vt_shape if n is None else jax.ShapeDtypeStruct((b, s, n), BF16) for n in widths],
        compiler_params=_params(2),
        name="in0_lat" if use_rope else "in0_ctx",
    )(x, g, scale, shift, w["wq"], w["wk"], w["wvt"], w["wu"], w["wvv"], w["wg"], w["lng"], w["lnb"], *tabs)


def _mix0_body(has_local, sink_ref, q_ref, *refs):
    if has_local:
        kp_ref, kc_ref, kn_ref, vp_ref, vc_ref, vn_ref, *refs = refs
    kx_ref, vx_ref, gu_ref, vnrm_ref, sg_ref, x_ref, gl_ref, ws_ref, bs_ref, wo_ref, o_ref = refs
    rows = q_ref.shape[1]
    qrows = A_BLOCK if has_local else rows
    nblk = rows // qrows
    npair = A_WIDTH // LANES
    ncol = npair * qrows
    nx = kx_ref.shape[1]
    lane_lo = lax.broadcasted_iota(jnp.int32, (qrows, LANES), 1) < A_HEAD_DIM
    chunk_lane_lo = lax.broadcasted_iota(jnp.int32, (A_BLOCK, LANES), 1) < A_HEAD_DIM
    if has_local:
        i = pl.program_id(1)
        nstep = pl.num_programs(1)
        key = lax.broadcasted_iota(jnp.int32, (A_BLOCK, ncol), 0)
        qry = lax.broadcasted_iota(jnp.int32, (A_BLOCK, ncol), 1) & (A_BLOCK - 1)
        band_prev = key >= qry
        band_next = key <= qry
        blk = lambda a: slice(a * A_BLOCK, (a + 1) * A_BLOCK)
        k_win = [kp_ref[0]] + [kc_ref[0, blk(a), :] for a in range(nblk)] + [kn_ref[0]]
        vt_win = [vp_ref[0]] + [vc_ref[0, :, blk(a)] for a in range(nblk)] + [vn_ref[0]]

    streams = []
    for a in range(nblk):
        if has_local:
            k_all = jnp.concatenate([kx_ref[0]] + k_win[a:a + 3], axis=0)
            vt_all = jnp.concatenate([vx_ref[0]] + vt_win[a:a + 3], axis=1)
        else:
            k_all, vt_all = kx_ref[0], vx_ref[0]
        q_pairs = [q_ref[0, a * qrows:(a + 1) * qrows, jp * LANES:(jp + 1) * LANES] for jp in range(npair)]
        for grp in range(A_KV_HEADS):
            keep = lane_lo if grp == 0 else ~lane_lo
            qm = jnp.concatenate([jnp.where(keep, qp, jnp.zeros_like(qp)) for qp in q_pairs], axis=0)
            streams.append((a, grp, _dot_nt(k_all, qm), vt_all))

    gmlp = []
    for jp in range(B_WIDTH // LANES):
        sl = slice(jp * LANES, (jp + 1) * LANES)
        chunks = []
        for ch in range(rows // A_BLOCK):
            rs = slice(ch * A_BLOCK, (ch + 1) * A_BLOCK)
            vp = vnrm_ref[0, rs, sl]
            s0 = _dot(ws_ref[2 * jp], vp)
            s1 = _dot(ws_ref[2 * jp + 1], vp)
            chunks.append(jnp.where(chunk_lane_lo, s0, s1) + bs_ref[:, sl])
        sg = chunks[0] if len(chunks) == 1 else jnp.concatenate(chunks, axis=0)
        gmlp.append(gu_ref[0, :, sl].astype(F32) * sg)

    out_t = {}
    for a, grp, s_t, vt_all in streams:
        if has_local:
            has_prev = (i > 0) if a == 0 else True
            has_next = (i < nstep - 1) if a == nblk - 1 else True
            parts = [s_t[:nx],
                     jnp.where(band_prev & has_prev, s_t[nx:nx + A_BLOCK], NEG_INF),
                     s_t[nx + A_BLOCK:nx + 2 * A_BLOCK],
                     jnp.where(band_next & has_next, s_t[nx + 2 * A_BLOCK:], NEG_INF)]
        else:
            parts = [s_t]
        sink = sink_ref[grp:grp + 1, :] * LOG2_E
        m = sink
        for part in parts:
            m = jnp.maximum(m, jnp.max(part, axis=0, keepdims=True))
        den = jnp.exp2(sink - m)
        probs = []
        for part in parts:
            p = jnp.exp2(part - m)
            den = den + jnp.sum(p, axis=0, keepdims=True)
            probs.append(p.astype(BF16))
        p_t = probs[0] if len(probs) == 1 else jnp.concatenate(probs, axis=0)
        out_t[a, grp] = _dot(vt_all, p_t) / den

    row_lo = lax.broadcasted_iota(jnp.int32, (LANES, qrows), 0) < A_HEAD_DIM
    pieces = []
    for jp in range(npair):
        cs = slice(jp * qrows, (jp + 1) * qrows)
        blocks = [jnp.where(row_lo, out_t[a, 0][:, cs], out_t[a, 1][:, cs]).T for a in range(nblk)]
        pieces.append(blocks[0] if nblk == 1 else jnp.concatenate(blocks, axis=0))
    pieces += gmlp

    mixed = jnp.concatenate(pieces, axis=-1) * sg_ref[0].astype(F32)
    y = _dot(mixed.astype(BF16), wo_ref[...])
    o_ref[0] = x_ref[0] + gl_ref[0] * y


MIX0_ROWS = 512


def _mix0_call(sink, q, k, vt, kx, vxt, gu, vnrm, sg, x, gate, ws, bs, wo, has_local):
    b, s, d = x.shape
    rows = MIX0_ROWS if has_local else s
    qrows = A_BLOCK if has_local else s
    nstep = s // rows
    nmod = gate.shape[0]
    mod_map = (lambda i, j: (i, 0, 0)) if nmod > 1 else (lambda i, j: (0, 0, 0))
    full = lambda a: pl.BlockSpec(a.shape, lambda i, j: (0,) * a.ndim)
    row = lambda n: pl.BlockSpec((1, rows, n), lambda i, j: (i, j, 0))
    ctx = lambda a: pl.BlockSpec((1,) + a.shape[1:], lambda i, j: (i, 0, 0))
    sink_rows = jnp.repeat(sink.reshape(A_KV_HEADS, A_HEADS // A_KV_HEADS), qrows, axis=1)
    in_specs = [full(sink_rows), row(A_WIDTH)]
    args = [sink_rows, q]
    if has_local:
        per = rows // A_BLOCK
        last = s // A_BLOCK - 1
        prev = lambda j: jnp.maximum(j * per - 1, 0)
        nxt = lambda j: jnp.minimum((j + 1) * per, last)
        kblk = lambda f: pl.BlockSpec((1, A_BLOCK, A_KV_WIDTH), lambda i, j: (i, f(j), 0))
        vblk = lambda f: pl.BlockSpec((1, A_KV_WIDTH, A_BLOCK), lambda i, j: (i, 0, f(j)))
        in_specs += [kblk(prev), row(A_KV_WIDTH), kblk(nxt),
                     vblk(prev), pl.BlockSpec((1, A_KV_WIDTH, rows), lambda i, j: (i, 0, j)), vblk(nxt)]
        args += [k, k, k, vt, vt, vt]
    in_specs += [ctx(kx), ctx(vxt), row(B_WIDTH), row(B_WIDTH), row(EVEN_MIX), row(d),
                 pl.BlockSpec((1, 1, d), mod_map), full(ws), full(bs), full(wo)]
    args += [kx, vxt, gu, vnrm, sg, x, gate, ws, bs, wo]
    return pl.pallas_call(
        functools.partial(_mix0_body, has_local),
        grid=(b, nstep),
        in_specs=in_specs,
        out_specs=row(d),
        out_shape=jax.ShapeDtypeStruct((b, s, d), F32),
        compiler_params=_params(2),
        name="mix0_lat" if has_local else "mix0_ctx",
    )(*args)


def _rms(x, g):
    return x * lax.rsqrt(jnp.mean(x * x, axis=-1, keepdims=True) + NORM_EPS) * g


def _in1_kv(kva, use_rope, kvn_ref, wk_ref, wvt_ref, tabs, k_ref, vt_ref):
    kvn = _rms(kva[:, :KV_RANK], kvn_ref[...]).astype(BF16)
    kpe = kva[:, KV_RANK:]
    if use_rope:
        kpe = _rope(kpe, *tabs, 8)
    kin = jnp.concatenate([kvn, kpe.astype(BF16)], axis=-1)
    k_ref[0] = _dot(kin, wk_ref[...]).astype(BF16)
    vt_ref[0] = _dot_nt(wvt_ref[...], kvn).astype(BF16)


def _in1_lat_body(x_ref, g_ref, sc_ref, sh_ref, wqa_ref, qn_ref, wqb_ref, wkv_ref, kvn_ref, wk_ref, wvt_ref,
                  wg_ref, cos_ref, shi_ref, slo_ref, cost_ref, sint_ref, qt_ref, k_ref, vt_ref, sg_ref):
    h = _modulated_rms(x_ref[0], g_ref[...], sc_ref[0], sh_ref[0]).astype(BF16)
    tabs = (cos_ref[...], shi_ref[...], slo_ref[...])
    scale = (C_NOPE + C_ROPE) ** -0.5 * LOG2_E
    qa = _dot(h, wqa_ref[...])
    kva = _dot(h, wkv_ref[...])
    gate = _dot(h, wg_ref[...])
    sg_ref[0] = (gate * jax.nn.sigmoid(gate)).astype(BF16)
    qn = _rms(qa, qn_ref[...] * scale).astype(BF16)
    half = C_ROPE // 4
    cos_t, sin_t = cost_ref[...], sint_ref[...]
    q_all = _dot_nt(wqb_ref[...], qn)
    for hd in range(C_HEADS):
        q_t = q_all[hd * LANES:(hd + 1) * LANES]
        parts = [q_t[:C_NOPE]]
        for ax in range(2):
            lo = C_NOPE + ax * 2 * half
            x1, x2 = q_t[lo:lo + half], q_t[lo + half:lo + 2 * half]
            c, sn = cos_t[ax * half:(ax + 1) * half], sin_t[ax * half:(ax + 1) * half]
            parts += [x1 * c - x2 * sn, x1 * sn + x2 * c]
        parts.append(q_t[C_NOPE + C_ROPE:])
        qt_ref[0, hd * LANES:(hd + 1) * LANES, :] = jnp.concatenate(parts, axis=0).astype(BF16)
    _in1_kv(kva, True, kvn_ref, wk_ref, wvt_ref, tabs, k_ref, vt_ref)


def _in1_ctx_body(x_ref, g_ref, sc_ref, sh_ref, wkv_ref, kvn_ref, wk_ref, wvt_ref, k_ref, vt_ref):
    h = _modulated_rms(x_ref[0], g_ref[...], sc_ref[0], sh_ref[0]).astype(BF16)
    _in1_kv(_dot(h, wkv_ref[...]), False, kvn_ref, wk_ref, wvt_ref, None, k_ref, vt_ref)


def _in1_call(x, g, scale, shift, w, tabs, tm, latent):
    b, s, d = x.shape
    nmod = scale.shape[0]
    mod_map = (lambda i, j: (i, 0, 0)) if nmod > 1 else (lambda i, j: (0, 0, 0))
    full = lambda a: pl.BlockSpec(a.shape, lambda i, j: (0,) * a.ndim)
    row = lambda n: pl.BlockSpec((1, tm, n), lambda i, j: (i, j, 0))
    tab = pl.BlockSpec((tm, LANES), lambda i, j: (j, 0))
    mod = pl.BlockSpec((1, 1, d), mod_map)
    kw = C_HEADS * LANES
    k_shape = jax.ShapeDtypeStruct((b, s, kw), BF16)
    vt_shape = jax.ShapeDtypeStruct((b, C_WIDTH, s), BF16)
    vt_spec = pl.BlockSpec((1, C_WIDTH, tm), lambda i, j: (i, 0, j))
    kv_w = [w["wkv"], w["kvn"], w["wk"], w["wvt"]]
    if latent:
        tab_t = pl.BlockSpec((C_ROPE // 2, tm), lambda i, j: (0, j))
        qt_spec = pl.BlockSpec((1, kw, tm), lambda i, j: (i, 0, j))
        return pl.pallas_call(
            _in1_lat_body,
            grid=(b, s // tm),
            in_specs=[row(d), full(g), mod, mod, full(w["wqa"]), full(w["qn"]), full(w["wqbt"])]
                     + [full(a) for a in kv_w] + [full(w["wg"]), tab, tab, tab, tab_t, tab_t],
            out_specs=[qt_spec, row(kw), vt_spec, row(C_WIDTH)],
            out_shape=[jax.ShapeDtypeStruct((b, kw, s), BF16), k_shape, vt_shape,
                       jax.ShapeDtypeStruct((b, s, C_WIDTH), BF16)],
            compiler_params=_params(2),
            name="in1_lat",
        )(x, g, scale, shift, w["wqa"], w["qn"], w["wqbt"], *kv_w, w["wg"], *tabs)
    return pl.pallas_call(
        _in1_ctx_body,
        grid=(b, s // tm),
        in_specs=[row(d), full(g), mod, mod] + [full(a) for a in kv_w],
        out_specs=[row(kw), vt_spec],
        out_shape=[k_shape, vt_shape],
        compiler_params=_params(2),
        name="in1_ctx",
    )(x, g, scale, shift, *kv_w)


ATTN1_KEY_CHUNK = 1024
ATTN1_HEADS = 8


def _attn1_body(qt_ref, k_ref, kx_ref, vt_ref, vxt_ref, sg_ref, o_ref):
    nh = ATTN1_HEADS
    tiles = []
    for kr, vr in ((k_ref, vt_ref), (kx_ref, vxt_ref)):
        n_keys = kr.shape[1]
        tiles += [(kr, vr, lo, min(lo + ATTN1_KEY_CHUNK, n_keys)) for lo in range(0, n_keys, ATTN1_KEY_CHUNK)]
    state = [None] * nh

    def scores(n, t):
        kr, _, lo, hi = tiles[n]
        return _dot(kr[0, lo:hi, t * LANES:(t + 1) * LANES],
                    qt_ref[0, t * LANES:(t + 1) * LANES, :])

    s_next = [scores(0, t) for t in range(nh)]
    for n, (_, vr, lo, hi) in enumerate(tiles):
        for t in range(nh):
            s = s_next[t]
            if n + 1 < len(tiles):
                s_next[t] = scores(n + 1, t)
            v_t = vr[0, t * C_V:(t + 1) * C_V, lo:hi]
            m_tile = jnp.max(s, axis=0, keepdims=True)
            if state[t] is None:
                m = m_tile
                p = jnp.exp2(s - m)
                den = jnp.sum(p, axis=0, keepdims=True)
                acc = _dot(v_t, p.astype(BF16))
            else:
                m_old, den_old, acc_old = state[t]
                m = jnp.maximum(m_old, m_tile)
                alpha = jnp.exp2(m_old - m)
                p = jnp.exp2(s - m)
                den = den_old * alpha + jnp.sum(p, axis=0, keepdims=True)
                acc = acc_old * alpha + _dot(v_t, p.astype(BF16))
            state[t] = (m, den, acc)
    for t in range(0, nh, 2):
        o = jnp.concatenate([acc / den for _, den, acc in state[t:t + 2]], axis=0).T
        sl = slice(t * C_V, (t + 2) * C_V)
        o_ref[0, :, sl] = (o * sg_ref[0, :, sl].astype(F32)).astype(BF16)


def _attn1_call(qt, k, kx, vt, vxt, sg, tq):
    b, s, _ = k.shape
    sx = kx.shape[1]
    nh = ATTN1_HEADS
    return pl.pallas_call(
        _attn1_body,
        grid=(b, C_HEADS // nh, s // tq),
        in_specs=[pl.BlockSpec((1, nh * LANES, tq), lambda i, h, j: (i, h, j)),
                  pl.BlockSpec((1, s, nh * LANES), lambda i, h, j: (i, 0, h)),
                  pl.BlockSpec((1, sx, nh * LANES), lambda i, h, j: (i, 0, h)),
                  pl.BlockSpec((1, nh * C_V, s), lambda i, h, j: (i, h, 0)),
                  pl.BlockSpec((1, nh * C_V, sx), lambda i, h, j: (i, h, 0)),
                  pl.BlockSpec((1, tq, nh * C_V), lambda i, h, j: (i, j, h))],
        out_specs=pl.BlockSpec((1, tq, nh * C_V), lambda i, h, j: (i, j, h)),
        out_shape=jax.ShapeDtypeStruct((b, s, C_WIDTH), BF16),
        compiler_params=_params(3),
        name="attn1",
    )(qt, k, kx, vt, vxt, sg)


def _out1_body(og_ref, x_ref, gl_ref, wo_ref, fg_ref, o_ref):
    x = x_ref[0] + gl_ref[0] * _dot(og_ref[0], wo_ref[...])
    o_ref[0] = _rms(x, fg_ref[...])


def _out1_call(og, x, gate, wo, fg, tm):
    b, s, d = x.shape
    full = lambda a: pl.BlockSpec(a.shape, lambda i, j: (0,) * a.ndim)
    row = lambda n: pl.BlockSpec((1, tm, n), lambda i, j: (i, j, 0))
    return pl.pallas_call(
        _out1_body,
        grid=(b, s // tm),
        in_specs=[row(C_WIDTH), row(d), pl.BlockSpec((1, 1, d), lambda i, j: (i, 0, 0)), full(wo), full(fg)],
        out_specs=row(d),
        out_shape=jax.ShapeDtypeStruct((b, s, d), F32),
        compiler_params=_params(2),
        name="out1",
    )(og, x, gate, wo, fg)


def _rope_tables(seq, offset, width, half):
    t = np.arange(seq)
    pos = np.stack([t // GRID_W, t % GRID_W], axis=0).astype(np.float32)
    lane = np.arange(LANES)
    d = (lane - offset) % (2 * width) if offset else lane % (2 * width)
    active = np.ones(LANES, bool) if not offset else (lane >= offset) & (lane < offset + 2 * width)
    axis = d // width
    e = d % width
    f = e % half
    inv = (ROPE_BASE ** (-(np.arange(half, dtype=np.float32)) / half)).astype(np.float32)
    ang = pos[axis, :].T * inv[f][None, :]
    cos = np.where(active[None, :], np.cos(ang), 1.0)
    sin = np.where(active[None, :], np.sin(ang), 0.0)
    hi = (e >= half)[None, :]
    return (jnp.asarray(cos, F32), jnp.asarray(np.where(hi, sin, 0.0), F32),
            jnp.asarray(np.where(hi, 0.0, -sin), F32))


def _rope_tables_t(seq, half):
    t = np.arange(seq)
    pos = np.stack([t // GRID_W, t % GRID_W], axis=0).astype(np.float32)
    inv = (ROPE_BASE ** (-(np.arange(half, dtype=np.float32)) / half)).astype(np.float32)
    ang = (pos[:, None, :] * inv[None, :, None]).reshape(2 * half, seq)
    return (jnp.asarray(np.cos(ang), F32), jnp.asarray(np.sin(ang), F32))


def _pair_heads(w, axis):
    shape = w.shape
    w = w.reshape(shape[:axis] + (A_KV_HEADS, A_HEADS // A_KV_HEADS, A_HEAD_DIM) + shape[axis + 1:])
    return jnp.swapaxes(w, axis, axis + 1).reshape(shape)


def kernel(x, c, ctx, c_ctx, norm_g, w_ada, b_ada, w_in0, sink0, gm_ln_g, gm_ln_b, gm_ws, gm_bs, w_out0, w_in1,
           q_norm, w_qb, kv_norm, w_kvb, w_out1, final_g):
    b, s, d = x.shape
    sx = ctx.shape[1]

    cond = jnp.concatenate([c, c_ctx[None, :], jnp.zeros((16 - b - 1, d), F32)], axis=0)
    mod = _ada_call(cond, w_ada, b_ada)
    parts = lambda l, r0, r1: [mod[l, r0:r1, i * d:(i + 1) * d][:, None, :] for i in range(3)]

    o = A_WIDTH
    w0 = w_in0[0].astype(BF16)
    wg0 = w0[:, o + 2 * A_KV_WIDTH + 2 * B_WIDTH:]
    wts0 = {
        "wq": _pair_heads(w0[:, :o], 1),
        "wk": w0[:, o:o + A_KV_WIDTH],
        "wvt": w0[:, o + A_KV_WIDTH:o + 2 * A_KV_WIDTH].T,
        "wu": w0[:, o + 2 * A_KV_WIDTH:o + 2 * A_KV_WIDTH + B_WIDTH],
        "wvv": w0[:, o + 2 * A_KV_WIDTH + B_WIDTH:o + 2 * A_KV_WIDTH + 2 * B_WIDTH],
        "wg": jnp.concatenate([_pair_heads(wg0[:, :A_WIDTH], 1), wg0[:, A_WIDTH:]], axis=1),
        "lng": gm_ln_g[0][None, :], "lnb": gm_ln_b[0][None, :],
    }
    tabs0 = _rope_tables(s, 0, A_HEAD_DIM // 2, A_HEAD_DIM // 4)
    g0 = norm_g[0][None, :]
    sh_l, sc_l, gt_l = parts(0, 0, b)
    sh_c, sc_c, gt_c = parts(0, b, b + 1)
    q, k, v, gu, vn, sg = _in0_call(x, g0, sc_l, sh_l, wts0, tabs0, 512, True)
    qc, kc, vc, guc, vnc, sgc = _in0_call(ctx, g0, sc_c, sh_c, wts0, tabs0, sx, False)

    ws = gm_ws[0].astype(BF16)
    bs = jnp.repeat(gm_bs[0].T, B_GROUP_DIM, axis=1)
    wo0 = w_out0[0].astype(BF16)
    wo0 = jnp.concatenate([_pair_heads(wo0[:A_WIDTH], 0), wo0[A_WIDTH:]], axis=0)
    sink = sink0[0]
    x1 = _mix0_call(sink, q, k, v, kc, vc, gu, vn, sg, x, gt_l, ws, bs, wo0, True)
    xc1 = _mix0_call(sink, qc, None, None, kc, vc, guc, vnc, sgc, ctx, gt_c, ws, bs, wo0, False)

    w1 = w_in1[0].astype(BF16)
    kvw = w_kvb[0].astype(BF16).reshape(KV_RANK, C_HEADS, C_NOPE + C_V)
    place = np.zeros((LANES, C_HEADS, LANES), np.float32)
    for e in range(C_ROPE):
        place[C_NOPE + e, :, C_NOPE + e] = 1.0
    wk = jnp.concatenate([jnp.pad(kvw[:, :, :C_NOPE], ((0, 0), (0, 0), (0, LANES - C_NOPE))),
                          jnp.asarray(place, BF16)], axis=0)
    wkv = jnp.concatenate([w1[:, Q_RANK:Q_RANK + KV_RANK], jnp.zeros((d, C_NOPE), BF16),
                           w1[:, Q_RANK + KV_RANK:Q_RANK + KV_RANK + C_ROPE],
                           jnp.zeros((d, LANES - C_NOPE - C_ROPE), BF16)], axis=1)
    wqb = jnp.pad(w_qb[0].astype(BF16).reshape(Q_RANK, C_HEADS, C_NOPE + C_ROPE),
                  ((0, 0), (0, 0), (0, LANES - C_NOPE - C_ROPE)))
    wts1 = {
        "wqa": w1[:, :Q_RANK], "qn": q_norm[0][None, :],
        "wqbt": wqb.reshape(Q_RANK, C_HEADS * LANES).T,
        "wkv": wkv, "kvn": kv_norm[0][None, :],
        "wk": wk.reshape(2 * LANES, C_HEADS * LANES),
        "wvt": kvw[:, :, C_NOPE:].reshape(KV_RANK, C_WIDTH).T,
        "wg": w1[:, Q_RANK + KV_RANK + C_ROPE:],
    }
    tabs1 = _rope_tables(s, C_NOPE, C_ROPE // 2, C_ROPE // 4) + _rope_tables_t(s, C_ROPE // 4)
    g1 = norm_g[1][None, :]
    sh_l, sc_l, gt_l = parts(1, 0, b)
    sh_c, sc_c, _ = parts(1, b, b + 1)
    q1, k1, vt1, sg1 = _in1_call(x1, g1, sc_l, sh_l, wts1, tabs1, 512, True)
    kx1, vxt1 = _in1_call(xc1, g1, sc_c, sh_c, wts1, None, sx, False)
    og = _attn1_call(q1, k1, kx1, vt1, vxt1, sg1, 256)
    return _out1_call(og, x1, gt_l, w_out1[0].astype(BF16), final_g[None, :], 512)
```

```python
import functools

import numpy as np
import jax
import jax.numpy as jnp
from jax import lax
from jax.experimental import pallas as pl
from jax.experimental.pallas import tpu as pltpu

D_MODEL = 1024
GRID_W = 64
ROPE_BASE = 10000.0
NORM_EPS = 1e-6
NEG_INF = -1e30

A_HEADS = 8
A_KV_HEADS = 2
A_HEAD_DIM = 64
A_WIDTH = A_HEADS * A_HEAD_DIM
A_KV_WIDTH = A_KV_HEADS * A_HEAD_DIM
A_BLOCK = 128
B_GROUPS = 8
B_WIDTH = 512
B_GROUP_DIM = 64
EVEN_MIX = A_WIDTH + B_WIDTH

C_HEADS = 16
C_NOPE = 64
C_ROPE = 32
C_V = 64
Q_RANK = 256
KV_RANK = 128
C_WIDTH = C_HEADS * C_V

LANES = 128
VMEM_LIMIT = 48 * 1024 * 1024

LOG2_E = 1.4426950408889634

BF16 = jnp.bfloat16
F32 = jnp.float32


def _params(n_axes):
    return pltpu.CompilerParams(dimension_semantics=("parallel",) * n_axes,
                                vmem_limit_bytes=VMEM_LIMIT)


def _dot(a, b):
    return jnp.dot(a, b, preferred_element_type=F32)


def _dot_nt(a, b):
    return lax.dot_general(a, b, (((1,), (1,)), ((), ())), preferred_element_type=F32)


def _rope(x, cos, sin_hi, sin_lo, half):
    return (x * cos + pltpu.roll(x, half, 1) * sin_hi
            + pltpu.roll(x, LANES - half, 1) * sin_lo)


def _gelu(x):
    return 0.5 * x * (1.0 + lax.erf(x * (2.0 ** -0.5)))


def _modulated_rms(x, g, scale, shift):
    r = lax.rsqrt(jnp.mean(x * x, axis=-1, keepdims=True) + NORM_EPS)
    return (x * r) * (g * (1.0 + scale)) + shift


def _ada_body(c_ref, w_ref, b_ref, o_ref):
    c = c_ref[...]
    cs = c * jax.nn.sigmoid(c)
    o_ref[0] = _dot(cs, w_ref[0]) + b_ref[0]


def _ada_call(cond, w_ada, b_ada):
    depth, d, n3 = w_ada.shape
    rows = cond.shape[0]
    tn = 768
    return pl.pallas_call(
        _ada_body,
        grid=(depth, n3 // tn),
        in_specs=[pl.BlockSpec((rows, d), lambda l, j: (0, 0)),
                  pl.BlockSpec((1, d, tn), lambda l, j: (l, 0, j)),
                  pl.BlockSpec((1, 1, tn), lambda l, j: (l, 0, j))],
        out_specs=pl.BlockSpec((1, rows, tn), lambda l, j: (l, 0, j)),
        out_shape=jax.ShapeDtypeStruct((depth, rows, n3), F32),
        compiler_params=_params(2),
        name="ada",
    )(cond, w_ada, b_ada.reshape(depth, 1, n3))


def _in0_body(use_rope, x_ref, g_ref, sc_ref, sh_ref, wq_ref, wk_ref, wvt_ref, wu_ref, wvv_ref, wg_ref,
              lng_ref, lnb_ref, cos_ref, shi_ref, slo_ref,
              q_ref, k_ref, vt_ref, gu_ref, vn_ref, sg_ref):
    h = _modulated_rms(x_ref[0], g_ref[...], sc_ref[0], sh_ref[0]).astype(BF16)
    if use_rope:
        cos, shi, slo = cos_ref[...], shi_ref[...], slo_ref[...]
        rope = lambda t: _rope(t, cos, shi, slo, 16)
    else:
        rope = lambda t: t
    scale = A_HEAD_DIM ** -0.5 * LOG2_E
    q = _dot(h, wq_ref[...])
    for j in range(A_WIDTH // LANES):
        sl = slice(j * LANES, (j + 1) * LANES)
        q_ref[0, :, sl] = (rope(q[:, sl]) * scale).astype(BF16)
    k_ref[0] = rope(_dot(h, wk_ref[...])).astype(BF16)
    vt_ref[0] = _dot_nt(wvt_ref[...], h).astype(BF16)
    gu_ref[0] = _gelu(_dot(h, wu_ref[...])).astype(BF16)
    gv = _gelu(_dot(h, wvv_ref[...]))
    mu = jnp.mean(gv, axis=-1, keepdims=True)
    gc = gv - mu
    vn = gc * lax.rsqrt(jnp.mean(gc * gc, axis=-1, keepdims=True) + NORM_EPS)
    vn_ref[0] = (vn * lng_ref[...] + lnb_ref[...]).astype(BF16)
    gate = _dot(h, wg_ref[...])
    sg_ref[0] = (gate * jax.nn.sigmoid(gate)).astype(BF16)


def _in0_call(x, g, scale, shift, w, tabs, tm, use_rope):
    b, s, d = x.shape
    nmod = scale.shape[0]
    mod_map = (lambda i, j: (i, 0, 0)) if nmod > 1 else (lambda i, j: (0, 0, 0))
    full = lambda a: pl.BlockSpec(a.shape, lambda i, j: (0,) * a.ndim)
    row = lambda n: pl.BlockSpec((1, tm, n), lambda i, j: (i, j, 0))
    tab = pl.BlockSpec((tm, LANES), lambda i, j: (j, 0))
    widths = [A_WIDTH, A_KV_WIDTH, None, B_WIDTH, B_WIDTH, EVEN_MIX]
    vt_spec = pl.BlockSpec((1, A_KV_WIDTH, tm), lambda i, j: (i, 0, j))
    vt_shape = jax.ShapeDtypeStruct((b, A_KV_WIDTH, s), BF16)
    return pl.pallas_call(
        functools.partial(_in0_body, use_rope),
        grid=(b, s // tm),
        in_specs=[row(d), full(g), pl.BlockSpec((1, 1, d), mod_map), pl.BlockSpec((1, 1, d), mod_map),
                  full(w["wq"]), full(w["wk"]), full(w["wvt"]), full(w["wu"]), full(w["wvv"]), full(w["wg"]),
                  full(w["lng"]), full(w["lnb"]), tab, tab, tab],
        out_specs=[vt_spec if n is None else row(n) for n in widths],
        out_shape=[vt_shape if n is None else jax.ShapeDtypeStruct((b, s, n), BF16) for n in widths],
        compiler_params=_params(2),
        name="in0_lat" if use_rope else "in0_ctx",
    )(x, g, scale, shift, w["wq"], w["wk"], w["wvt"], w["wu"], w["wvv"], w["wg"], w["lng"], w["lnb"], *tabs)


def _mix0_body(has_local, sink_ref, q_ref, *refs):
    if has_local:
        kp_ref, kc_ref, kn_ref, vp_ref, vc_ref, vn_ref, *refs = refs
    kx_ref, vx_ref, gu_ref, vnrm_ref, sg_ref, x_ref, gl_ref, ws_ref, bs_ref, wo_ref, o_ref = refs
    rows = q_ref.shape[1]
    qrows = A_BLOCK if has_local else rows
    nblk = rows // qrows
    npair = A_WIDTH // LANES
    ncol = npair * qrows
    nx = kx_ref.shape[1]
    lane_lo = lax.broadcasted_iota(jnp.int32, (qrows, LANES), 1) < A_HEAD_DIM
    chunk_lane_lo = lax.broadcasted_iota(jnp.int32, (A_BLOCK, LANES), 1) < A_HEAD_DIM
    if has_local:
        i = pl.program_id(1)
        nstep = pl.num_programs(1)
        key = lax.broadcasted_iota(jnp.int32, (A_BLOCK, ncol), 0)
        qry = lax.broadcasted_iota(jnp.int32, (A_BLOCK, ncol), 1) & (A_BLOCK - 1)
        band_prev = key >= qry
        band_next = key <= qry
        blk = lambda a: slice(a * A_BLOCK, (a + 1) * A_BLOCK)
        k_win = [kp_ref[0]] + [kc_ref[0, blk(a), :] for a in range(nblk)] + [kn_ref[0]]
        vt_win = [vp_ref[0]] + [vc_ref[0, :, blk(a)] for a in range(nblk)] + [vn_ref[0]]

    streams = []
    for a in range(nblk):
        if has_local:
            k_all = jnp.concatenate([kx_ref[0]] + k_win[a:a + 3], axis=0)
            vt_all = jnp.concatenate([vx_ref[0]] + vt_win[a:a + 3], axis=1)
        else:
            k_all, vt_all = kx_ref[0], vx_ref[0]
        q_pairs = [q_ref[0, a * qrows:(a + 1) * qrows, jp * LANES:(jp + 1) * LANES] for jp in range(npair)]
        for grp in range(A_KV_HEADS):
            keep = lane_lo if grp == 0 else ~lane_lo
            qm = jnp.concatenate([jnp.where(keep, qp, jnp.zeros_like(qp)) for qp in q_pairs], axis=0)
            streams.append((a, grp, _dot_nt(k_all, qm), vt_all))

    gmlp = []
    for jp in range(B_WIDTH // LANES):
        sl = slice(jp * LANES, (jp + 1) * LANES)
        chunks = []
        for ch in range(rows // A_BLOCK):
            rs = slice(ch * A_BLOCK, (ch + 1) * A_BLOCK)
            vp = vnrm_ref[0, rs, sl]
            zero = jnp.zeros_like(vp)
            rhs = jnp.concatenate([jnp.where(chunk_lane_lo, vp, zero), jnp.where(chunk_lane_lo, zero, vp)], axis=0)
            chunks.append(_dot(ws_ref[jp], rhs) + bs_ref[:, sl])
        sg = chunks[0] if len(chunks) == 1 else jnp.concatenate(chunks, axis=0)
        gmlp.append(gu_ref[0, :, sl].astype(F32) * sg)

    out_t = {}
    for a, grp, s_t, vt_all in streams:
        if has_local:
            has_prev = (i > 0) if a == 0 else True
            has_next = (i < nstep - 1) if a == nblk - 1 else True
            parts = [s_t[:nx],
                     jnp.where(band_prev & has_prev, s_t[nx:nx + A_BLOCK], NEG_INF),
                     s_t[nx + A_BLOCK:nx + 2 * A_BLOCK],
                     jnp.where(band_next & has_next, s_t[nx + 2 * A_BLOCK:], NEG_INF)]
        else:
            parts = [s_t]
        sink = sink_ref[grp:grp + 1, :] * LOG2_E
        m = sink
        for part in parts:
            m = jnp.maximum(m, jnp.max(part, axis=0, keepdims=True))
        den = jnp.exp2(sink - m)
        probs = []
        for part in parts:
            p = jnp.exp2(part - m)
            den = den + jnp.sum(p, axis=0, keepdims=True)
            probs.append(p.astype(BF16))
        p_t = probs[0] if len(probs) == 1 else jnp.concatenate(probs, axis=0)
        out_t[a, grp] = _dot(vt_all, p_t) / den

    row_lo = lax.broadcasted_iota(jnp.int32, (LANES, qrows), 0) < A_HEAD_DIM
    pieces = []
    for jp in range(npair):
        cs = slice(jp * qrows, (jp + 1) * qrows)
        blocks = [jnp.where(row_lo, out_t[a, 0][:, cs], out_t[a, 1][:, cs]).T for a in range(nblk)]
        pieces.append(blocks[0] if nblk == 1 else jnp.concatenate(blocks, axis=0))
    pieces += gmlp

    mixed = jnp.concatenate(pieces, axis=-1) * sg_ref[0].astype(F32)
    y = _dot(mixed.astype(BF16), wo_ref[...])
    o_ref[0] = x_ref[0] + gl_ref[0] * y


MIX0_ROWS = 512


def _mix0_call(sink, q, k, vt, kx, vxt, gu, vnrm, sg, x, gate, ws, bs, wo, has_local):
    b, s, d = x.shape
    rows = MIX0_ROWS if has_local else s
    qrows = A_BLOCK if has_local else s
    nstep = s // rows
    nmod = gate.shape[0]
    mod_map = (lambda i, j: (i, 0, 0)) if nmod > 1 else (lambda i, j: (0, 0, 0))
    full = lambda a: pl.BlockSpec(a.shape, lambda i, j: (0,) * a.ndim)
    row = lambda n: pl.BlockSpec((1, rows, n), lambda i, j: (i, j, 0))
    ctx = lambda a: pl.BlockSpec((1,) + a.shape[1:], lambda i, j: (i, 0, 0))
    sink_rows = jnp.repeat(sink.reshape(A_KV_HEADS, A_HEADS // A_KV_HEADS), qrows, axis=1)
    in_specs = [full(sink_rows), row(A_WIDTH)]
    args = [sink_rows, q]
    if has_local:
        per = rows // A_BLOCK
        last = s // A_BLOCK - 1
        prev = lambda j: jnp.maximum(j * per - 1, 0)
        nxt = lambda j: jnp.minimum((j + 1) * per, last)
        kblk = lambda f: pl.BlockSpec((1, A_BLOCK, A_KV_WIDTH), lambda i, j: (i, f(j), 0))
        vblk = lambda f: pl.BlockSpec((1, A_KV_WIDTH, A_BLOCK), lambda i, j: (i, 0, f(j)))
        in_specs += [kblk(prev), row(A_KV_WIDTH), kblk(nxt),
                     vblk(prev), pl.BlockSpec((1, A_KV_WIDTH, rows), lambda i, j: (i, 0, j)), vblk(nxt)]
        args += [k, k, k, vt, vt, vt]
    in_specs += [ctx(kx), ctx(vxt), row(B_WIDTH), row(B_WIDTH), row(EVEN_MIX), row(d),
                 pl.BlockSpec((1, 1, d), mod_map), full(ws), full(bs), full(wo)]
    args += [kx, vxt, gu, vnrm, sg, x, gate, ws, bs, wo]
    return pl.pallas_call(
        functools.partial(_mix0_body, has_local),
        grid=(b, nstep),
        in_specs=in_specs,
        out_specs=row(d),
        out_shape=jax.ShapeDtypeStruct((b, s, d), F32),
        compiler_params=_params(2),
        name="mix0_lat" if has_local else "mix0_ctx",
    )(*args)


def _rms(x, g):
    return x * lax.rsqrt(jnp.mean(x * x, axis=-1, keepdims=True) + NORM_EPS) * g


def _in1_kv(kva, use_rope, kvn_ref, wk_ref, wvt_ref, tabs, k_ref, vt_ref):
    kvn = _rms(kva[:, :KV_RANK], kvn_ref[...]).astype(BF16)
    kpe = kva[:, KV_RANK:]
    if use_rope:
        kpe = _rope(kpe, *tabs, 8)
    kin = jnp.concatenate([kvn, kpe.astype(BF16)], axis=-1)
    k_ref[0] = _dot(kin, wk_ref[...]).astype(BF16)
    vt_ref[0] = _dot_nt(wvt_ref[...], kvn).astype(BF16)


def _in1_lat_body(x_ref, g_ref, sc_ref, sh_ref, wqa_ref, qn_ref, wqb_ref, wkv_ref, kvn_ref, wk_ref, wvt_ref,
                  wg_ref, cos_ref, shi_ref, slo_ref, cost_ref, sint_ref, qt_ref, k_ref, vt_ref, sg_ref):
    h = _modulated_rms(x_ref[0], g_ref[...], sc_ref[0], sh_ref[0]).astype(BF16)
    tabs = (cos_ref[...], shi_ref[...], slo_ref[...])
    scale = (C_NOPE + C_ROPE) ** -0.5 * LOG2_E
    qa = _dot(h, wqa_ref[...])
    kva = _dot(h, wkv_ref[...])
    gate = _dot(h, wg_ref[...])
    sg_ref[0] = (gate * jax.nn.sigmoid(gate)).astype(BF16)
    qn = _rms(qa, qn_ref[...] * scale).astype(BF16)
    half = C_ROPE // 4
    cos_t, sin_t = cost_ref[...], sint_ref[...]
    q_all = _dot_nt(wqb_ref[...], qn)
    for hd in range(C_HEADS):
        q_t = q_all[hd * LANES:(hd + 1) * LANES]
        parts = [q_t[:C_NOPE]]
        for ax in range(2):
            lo = C_NOPE + ax * 2 * half
            x1, x2 = q_t[lo:lo + half], q_t[lo + half:lo + 2 * half]
            c, sn = cos_t[ax * half:(ax + 1) * half], sin_t[ax * half:(ax + 1) * half]
            parts += [x1 * c - x2 * sn, x1 * sn + x2 * c]
        parts.append(q_t[C_NOPE + C_ROPE:])
        qt_ref[0, hd * LANES:(hd + 1) * LANES, :] = jnp.concatenate(parts, axis=0).astype(BF16)
    _in1_kv(kva, True, kvn_ref, wk_ref, wvt_ref, tabs, k_ref, vt_ref)


def _in1_ctx_body(x_ref, g_ref, sc_ref, sh_ref, wkv_ref, kvn_ref, wk_ref, wvt_ref, k_ref, vt_ref):
    h = _modulated_rms(x_ref[0], g_ref[...], sc_ref[0], sh_ref[0]).astype(BF16)
    _in1_kv(_dot(h, wkv_ref[...]), False, kvn_ref, wk_ref, wvt_ref, None, k_ref, vt_ref)


def _in1_call(x, g, scale, shift, w, tabs, tm, latent):
    b, s, d = x.shape
    nmod = scale.shape[0]
    mod_map = (lambda i, j: (i, 0, 0)) if nmod > 1 else (lambda i, j: (0, 0, 0))
    full = lambda a: pl.BlockSpec(a.shape, lambda i, j: (0,) * a.ndim)
    row = lambda n: pl.BlockSpec((1, tm, n), lambda i, j: (i, j, 0))
    tab = pl.BlockSpec((tm, LANES), lambda i, j: (j, 0))
    mod = pl.BlockSpec((1, 1, d), mod_map)
    kw = C_HEADS * LANES
    k_shape = jax.ShapeDtypeStruct((b, s, kw), BF16)
    vt_shape = jax.ShapeDtypeStruct((b, C_WIDTH, s), BF16)
    vt_spec = pl.BlockSpec((1, C_WIDTH, tm), lambda i, j: (i, 0, j))
    kv_w = [w["wkv"], w["kvn"], w["wk"], w["wvt"]]
    if latent:
        tab_t = pl.BlockSpec((C_ROPE // 2, tm), lambda i, j: (0, j))
        qt_spec = pl.BlockSpec((1, kw, tm), lambda i, j: (i, 0, j))
        return pl.pallas_call(
            _in1_lat_body,
            grid=(b, s // tm),
            in_specs=[row(d), full(g), mod, mod, full(w["wqa"]), full(w["qn"]), full(w["wqbt"])]
                     + [full(a) for a in kv_w] + [full(w["wg"]), tab, tab, tab, tab_t, tab_t],
            out_specs=[qt_spec, row(kw), vt_spec, row(C_WIDTH)],
            out_shape=[jax.ShapeDtypeStruct((b, kw, s), BF16), k_shape, vt_shape,
                       jax.ShapeDtypeStruct((b, s, C_WIDTH), BF16)],
            compiler_params=_params(2),
            name="in1_lat",
        )(x, g, scale, shift, w["wqa"], w["qn"], w["wqbt"], *kv_w, w["wg"], *tabs)
    return pl.pallas_call(
        _in1_ctx_body,
        grid=(b, s // tm),
        in_specs=[row(d), full(g), mod, mod] + [full(a) for a in kv_w],
        out_specs=[row(kw), vt_spec],
        out_shape=[k_shape, vt_shape],
        compiler_params=_params(2),
        name="in1_ctx",
    )(x, g, scale, shift, *kv_w)


ATTN1_KEY_CHUNK = 512
ATTN1_HEADS = 8


def _attn1_body(qt_ref, k_ref, kx_ref, vt_ref, vxt_ref, sg_ref, o_ref):
    nh = ATTN1_HEADS
    tiles = []
    for kr, vr in ((k_ref, vt_ref), (kx_ref, vxt_ref)):
        n_keys = kr.shape[1]
        tiles += [(kr, vr, lo, min(lo + ATTN1_KEY_CHUNK, n_keys)) for lo in range(0, n_keys, ATTN1_KEY_CHUNK)]
    state = [None] * nh

    def scores(n, t):
        kr, _, lo, hi = tiles[n]
        return _dot(kr[0, lo:hi, t * LANES:(t + 1) * LANES],
                    qt_ref[0, t * LANES:(t + 1) * LANES, :])

    s_next = [scores(0, t) for t in range(nh)]
    for n, (_, vr, lo, hi) in enumerate(tiles):
        for t in range(nh):
            s = s_next[t]
            if n + 1 < len(tiles):
                s_next[t] = scores(n + 1, t)
            v_t = vr[0, t * C_V:(t + 1) * C_V, lo:hi]
            m_tile = jnp.max(s, axis=0, keepdims=True)
            if state[t] is None:
                m = m_tile
                p = jnp.exp2(s - m)
                den = jnp.sum(p, axis=0, keepdims=True)
                acc = _dot(v_t, p.astype(BF16))
            else:
                m_old, den_old, acc_old = state[t]
                m = jnp.maximum(m_old, m_tile)
                alpha = jnp.exp2(m_old - m)
                p = jnp.exp2(s - m)
                den = den_old * alpha + jnp.sum(p, axis=0, keepdims=True)
                acc = acc_old * alpha + _dot(v_t, p.astype(BF16))
            state[t] = (m, den, acc)
    for t in range(0, nh, 2):
        o = jnp.concatenate([acc / den for _, den, acc in state[t:t + 2]], axis=0).T
        sl = slice(t * C_V, (t + 2) * C_V)
        o_ref[0, :, sl] = (o * sg_ref[0, :, sl].astype(F32)).astype(BF16)


def _attn1_call(qt, k, kx, vt, vxt, sg, tq):
    b, s, _ = k.shape
    sx = kx.shape[1]
    nh = ATTN1_HEADS
    return pl.pallas_call(
        _attn1_body,
        grid=(b, C_HEADS // nh, s // tq),
        in_specs=[pl.BlockSpec((1, nh * LANES, tq), lambda i, h, j: (i, h, j)),
                  pl.BlockSpec((1, s, nh * LANES), lambda i, h, j: (i, 0, h)),
                  pl.BlockSpec((1, sx, nh * LANES), lambda i, h, j: (i, 0, h)),
                  pl.BlockSpec((1, nh * C_V, s), lambda i, h, j: (i, h, 0)),
                  pl.BlockSpec((1, nh * C_V, sx), lambda i, h, j: (i, h, 0)),
                  pl.BlockSpec((1, tq, nh * C_V), lambda i, h, j: (i, j, h))],
        out_specs=pl.BlockSpec((1, tq, nh * C_V), lambda i, h, j: (i, j, h)),
        out_shape=jax.ShapeDtypeStruct((b, s, C_WIDTH), BF16),
        compiler_params=_params(3),
        name="attn1",
    )(qt, k, kx, vt, vxt, sg)


def _out1_body(og_ref, x_ref, gl_ref, wo_ref, fg_ref, o_ref):
    x = x_ref[0] + gl_ref[0] * _dot(og_ref[0], wo_ref[...])
    o_ref[0] = _rms(x, fg_ref[...])


def _out1_call(og, x, gate, wo, fg, tm):
    b, s, d = x.shape
    full = lambda a: pl.BlockSpec(a.shape, lambda i, j: (0,) * a.ndim)
    row = lambda n: pl.BlockSpec((1, tm, n), lambda i, j: (i, j, 0))
    return pl.pallas_call(
        _out1_body,
        grid=(b, s // tm),
        in_specs=[row(C_WIDTH), row(d), pl.BlockSpec((1, 1, d), lambda i, j: (i, 0, 0)), full(wo), full(fg)],
        out_specs=row(d),
        out_shape=jax.ShapeDtypeStruct((b, s, d), F32),
        compiler_params=_params(2),
        name="out1",
    )(og, x, gate, wo, fg)


def _rope_tables(seq, offset, width, half):
    t = np.arange(seq)
    pos = np.stack([t // GRID_W, t % GRID_W], axis=0).astype(np.float32)
    lane = np.arange(LANES)
    d = (lane - offset) % (2 * width) if offset else lane % (2 * width)
    active = np.ones(LANES, bool) if not offset else (lane >= offset) & (lane < offset + 2 * width)
    axis = d // width
    e = d % width
    f = e % half
    inv = (ROPE_BASE ** (-(np.arange(half, dtype=np.float32)) / half)).astype(np.float32)
    ang = pos[axis, :].T * inv[f][None, :]
    cos = np.where(active[None, :], np.cos(ang), 1.0)
    sin = np.where(active[None, :], np.sin(ang), 0.0)
    hi = (e >= half)[None, :]
    return (jnp.asarray(cos, F32), jnp.asarray(np.where(hi, sin, 0.0), F32),
            jnp.asarray(np.where(hi, 0.0, -sin), F32))


def _rope_tables_t(seq, half):
    t = np.arange(seq)
    pos = np.stack([t // GRID_W, t % GRID_W], axis=0).astype(np.float32)
    inv = (ROPE_BASE ** (-(np.arange(half, dtype=np.float32)) / half)).astype(np.float32)
    ang = (pos[:, None, :] * inv[None, :, None]).reshape(2 * half, seq)
    return (jnp.asarray(np.cos(ang), F32), jnp.asarray(np.sin(ang), F32))


def _pair_heads(w, axis):
    shape = w.shape
    w = w.reshape(shape[:axis] + (A_KV_HEADS, A_HEADS // A_KV_HEADS, A_HEAD_DIM) + shape[axis + 1:])
    return jnp.swapaxes(w, axis, axis + 1).reshape(shape)


def kernel(x, c, ctx, c_ctx, norm_g, w_ada, b_ada, w_in0, sink0, gm_ln_g, gm_ln_b, gm_ws, gm_bs, w_out0, w_in1,
           q_norm, w_qb, kv_norm, w_kvb, w_out1, final_g):
    b, s, d = x.shape
    sx = ctx.shape[1]

    cond = jnp.concatenate([c, c_ctx[None, :], jnp.zeros((16 - b - 1, d), F32)], axis=0)
    mod = _ada_call(cond, w_ada, b_ada)
    parts = lambda l, r0, r1: [mod[l, r0:r1, i * d:(i + 1) * d][:, None, :] for i in range(3)]

    o = A_WIDTH
    w0 = w_in0[0].astype(BF16)
    wg0 = w0[:, o + 2 * A_KV_WIDTH + 2 * B_WIDTH:]
    wts0 = {
        "wq": _pair_heads(w0[:, :o], 1),
        "wk": w0[:, o:o + A_KV_WIDTH],
        "wvt": w0[:, o + A_KV_WIDTH:o + 2 * A_KV_WIDTH].T,
        "wu": w0[:, o + 2 * A_KV_WIDTH:o + 2 * A_KV_WIDTH + B_WIDTH],
        "wvv": w0[:, o + 2 * A_KV_WIDTH + B_WIDTH:o + 2 * A_KV_WIDTH + 2 * B_WIDTH],
        "wg": jnp.concatenate([_pair_heads(wg0[:, :A_WIDTH], 1), wg0[:, A_WIDTH:]], axis=1),
        "lng": gm_ln_g[0][None, :], "lnb": gm_ln_b[0][None, :],
    }
    tabs0 = _rope_tables(s, 0, A_HEAD_DIM // 2, A_HEAD_DIM // 4)
    g0 = norm_g[0][None, :]
    sh_l, sc_l, gt_l = parts(0, 0, b)
    sh_c, sc_c, gt_c = parts(0, b, b + 1)
    q, k, v, gu, vn, sg = _in0_call(x, g0, sc_l, sh_l, wts0, tabs0, 512, True)
    qc, kc, vc, guc, vnc, sgc = _in0_call(ctx, g0, sc_c, sh_c, wts0, tabs0, sx, False)

    ws = gm_ws[0].astype(BF16)
    ws = jnp.concatenate([ws[0::2], ws[1::2]], axis=2)
    bs = jnp.repeat(gm_bs[0].T, B_GROUP_DIM, axis=1)
    wo0 = w_out0[0].astype(BF16)
    wo0 = jnp.concatenate([_pair_heads(wo0[:A_WIDTH], 0), wo0[A_WIDTH:]], axis=0)
    sink = sink0[0]
    x1 = _mix0_call(sink, q, k, v, kc, vc, gu, vn, sg, x, gt_l, ws, bs, wo0, True)
    xc1 = _mix0_call(sink, qc, None, None, kc, vc, guc, vnc, sgc, ctx, gt_c, ws, bs, wo0, False)

    w1 = w_in1[0].astype(BF16)
    kvw = w_kvb[0].astype(BF16).reshape(KV_RANK, C_HEADS, C_NOPE + C_V)
    place = np.zeros((LANES, C_HEADS, LANES), np.float32)
    for e in range(C_ROPE):
        place[C_NOPE + e, :, C_NOPE + e] = 1.0
    wk = jnp.concatenate([jnp.pad(kvw[:, :, :C_NOPE], ((0, 0), (0, 0), (0, LANES - C_NOPE))),
                          jnp.asarray(place, BF16)], axis=0)
    wkv = jnp.concatenate([w1[:, Q_RANK:Q_RANK + KV_RANK], jnp.zeros((d, C_NOPE), BF16),
                           w1[:, Q_RANK + KV_RANK:Q_RANK + KV_RANK + C_ROPE],
                           jnp.zeros((d, LANES - C_NOPE - C_ROPE), BF16)], axis=1)
    wqb = jnp.pad(w_qb[0].astype(BF16).reshape(Q_RANK, C_HEADS, C_NOPE + C_ROPE),
                  ((0, 0), (0, 0), (0, LANES - C_NOPE - C_ROPE)))
    wts1 = {
        "wqa": w1[:, :Q_RANK], "qn": q_norm[0][None, :],
        "wqbt": wqb.reshape(Q_RANK, C_HEADS * LANES).T,
        "wkv": wkv, "kvn": kv_norm[0][None, :],
        "wk": wk.reshape(2 * LANES, C_HEADS * LANES),
        "wvt": kvw[:, :, C_NOPE:].reshape(KV_RANK, C_WIDTH).T,
        "wg": w1[:, Q_RANK + KV_RANK + C_ROPE:],
    }
    tabs1 = _rope_tables(s, C_NOPE, C_ROPE // 2, C_ROPE // 4) + _rope_tables_t(s, C_ROPE // 4)
    g1 = norm_g[1][None, :]
    sh_l, sc_l, gt_l = parts(1, 0, b)
    sh_c, sc_c, _ = parts(1, b, b + 1)
    q1, k1, vt1, sg1 = _in1_call(x1, g1, sc_l, sh_l, wts1, tabs1, 512, True)
    kx1, vxt1 = _in1_call(xc1, g1, sc_c, sh_c, wts1, None, sx, False)
    og = _attn1_call(q1, k1, kx1, vt1, vxt1, sg1, 512)
    return _out1_call(og, x1, gt_l, w_out1[0].astype(BF16), final_g[None, :], 1024)
```
